```python
import math
import jax, jax.numpy as jnp
from jax import lax
import numpy as np

D_MODEL = 1024
BATCH = 4
SEQ = 4096
DEPTH = 2

N_MIXERS = 2
N_RWKV = (DEPTH + 1) // 2
N_CONV = DEPTH // 2
RWKV_HEAD = 64
RWKV_HEADS = D_MODEL // RWKV_HEAD
D_DECAY_LORA = 64
D_AAA_LORA = 64
D_GATE_LORA = 128
N_LERP = 6
LNX_EPS = RWKV_HEAD * 1e-5
CONV_WIDTH = 31
CONV_LN_EPS = 1e-5
N_GROUPS = 4
EXPERTS_PER_GROUP = 8
N_EXPERTS = N_GROUPS * EXPERTS_PER_GROUP
TOP_K = 2
D_EXPERT = D_MODEL // 4
RMS_EPS = 1e-6

kernel_name = "hybrid_rwkv7_conformer_hmoe"


def rms_norm(x, g):
    xf = x.astype(jnp.float32)
    y = xf * lax.rsqrt(jnp.mean(xf * xf, axis=-1, keepdims=True) + RMS_EPS)
    return (y * g.astype(jnp.float32)).astype(x.dtype)


def token_shift(x):
    return jnp.pad(x[:, :-1], ((0, 0), (1, 0), (0, 0)))


def rwkv7_recurrence(r, decay, k, v, aa, bb):
    B, T, H, N = r.shape

    def step(S, inp):
        r_t, w_t, k_t, v_t, a_t, b_t = inp
        sa = jnp.einsum('bhij,bhj->bhi', S, a_t)
        S = S * w_t[:, :, None, :] + sa[..., None] * b_t[:, :, None, :] + v_t[..., None] * k_t[:, :, None, :]
        y = jnp.einsum('bhij,bhj->bhi', S, r_t)
        return S, y

    xs = tuple(jnp.moveaxis(t, 1, 0) for t in (r, decay, k, v, aa, bb))
    S0 = jnp.zeros((B, H, N, N), jnp.float32)
    _, y = lax.scan(step, S0, xs)
    return jnp.moveaxis(y, 0, 1)


def rwkv7_time_mix(x, mu, w_rkv, w0, w1, w2, a0, a1, a2, g1, g2, k_k, k_a, r_k, ln_g, ln_b, w_o):
    B, T, C = x.shape
    H, N = RWKV_HEADS, RWKV_HEAD
    xx = token_shift(x) - x
    xs = x[None] + xx[None] * mu[:, None, None, :]
    xr, xw, xk, xv, xa, xg = xs[0], xs[1], xs[2], xs[3], xs[4], xs[5]
    rkv = jnp.einsum('nbtc,ncd->nbtd', jnp.stack([xr, xk, xv]), w_rkv)
    r, k, v = rkv[0], rkv[1], rkv[2]
    w = -jax.nn.softplus(-(w0 + jnp.tanh(xw @ w1) @ w2)) - 0.5
    a = jax.nn.sigmoid(a0 + (xa @ a1) @ a2)
    g = jax.nn.sigmoid(xg @ g1) @ g2
    f32 = jnp.float32
    kk = (k * k_k).astype(f32).reshape(B, T, H, N)
    kk = kk / jnp.maximum(jnp.linalg.norm(kk, axis=-1, keepdims=True), 1e-12)
    k = k * (1.0 + (a - 1.0) * k_a)
    rh = r.astype(f32).reshape(B, T, H, N)
    kh = k.astype(f32).reshape(B, T, H, N)
    vh = v.astype(f32).reshape(B, T, H, N)
    ah = a.astype(f32).reshape(B, T, H, N)
    decay = jnp.exp(-jnp.exp(w.astype(f32))).reshape(B, T, H, N)
    y = rwkv7_recurrence(rh, decay, kh, vh, -kk, kk * ah)
    mean = jnp.mean(y, axis=-1, keepdims=True)
    var = jnp.mean(jnp.square(y - mean), axis=-1, keepdims=True)
    y = ((y - mean) * lax.rsqrt(var + LNX_EPS)).reshape(B, T, C) * ln_g.astype(f32) + ln_b.astype(f32)
    bonus = jnp.sum(rh * kh * r_k.astype(f32), axis=-1, keepdims=True) * vh
    y = (y + bonus.reshape(B, T, C)).astype(x.dtype)
    return (y * g) @ w_o


def conformer_conv(x, w_pw1, b_pw1, w_dw, b_dw, ln_g, ln_b, w_pw2, b_pw2):
    u = x @ w_pw1 + b_pw1
    u = u[..., :D_MODEL] * jax.nn.sigmoid(u[..., D_MODEL:])
    u = lax.conv_general_dilated(
        u, w_dw[:, None, :].astype(u.dtype), window_strides=(1,),
        padding=[(CONV_WIDTH - 1, 0)], dimension_numbers=('NWC', 'WIO', 'NWC'),
        feature_group_count=D_MODEL) + b_dw
    uf = u.astype(jnp.float32)
    m = jnp.mean(uf, axis=-1, keepdims=True)
    var = jnp.mean(jnp.square(uf - m), axis=-1, keepdims=True)
    u = ((uf - m) * lax.rsqrt(var + CONV_LN_EPS) * ln_g.astype(jnp.float32) + ln_b.astype(jnp.float32)).astype(x.dtype)
    u = jax.nn.silu(u)
    return u @ w_pw2 + b_pw2


def hierarchical_moe(h, w_group, b_group, w_expert, b_expert, w_gate, w_up, w_down):
    B, T, D = h.shape
    ht = h.reshape(-1, D)
    f32 = jnp.float32
    gp = jax.nn.softmax((ht @ w_group + b_group).astype(f32), axis=-1)
    p_top, g_idx = lax.top_k(gp, 1)
    el = (ht @ w_expert + b_expert).astype(f32).reshape(-1, N_GROUPS, EXPERTS_PER_GROUP)
    el_sel = jnp.take_along_axis(el, g_idx[:, :, None], axis=1)[:, 0]
    top_l, e_idx = lax.top_k(el_sel, TOP_K)
    w_in = jax.nn.softmax(top_l, axis=-1) * p_top
    within = jnp.sum(jax.nn.one_hot(e_idx, EXPERTS_PER_GROUP, dtype=f32) * w_in[..., None], axis=1)
    gates = (jax.nn.one_hot(g_idx[:, 0], N_GROUPS, dtype=f32)[:, :, None] * within[:, None, :])
    gates = gates.reshape(-1, N_EXPERTS).astype(h.dtype)

    def expert_step(acc, p):
        wg, wu, wd, ge = p
        z = jax.nn.silu(ht @ wg) * (ht @ wu)
        return acc + ge[:, None] * (z @ wd), None

    y, _ = lax.scan(expert_step, jnp.zeros_like(ht), (w_gate, w_up, w_down, gates.T))
    return y.reshape(B, T, D)


def setup_inputs(seed: int = 0) -> dict:
    key = jax.random.key(seed)
    ks = iter(jax.random.split(key, 64))
    D, f = D_MODEL, jnp.float32

    def nrm(shape, scale):
        return jax.random.normal(next(ks), shape, f) * scale

    def unif(shape, lo, hi):
        return jax.random.uniform(next(ks), shape, f, lo, hi)

    n_idx = jnp.arange(D, dtype=f) / (D - 1)
    w0_base = -7.0 + 5.0 * n_idx ** 0.9 + 0.5
    inp = {}
    inp["x"] = nrm((BATCH, SEQ, D), 1.0)
    inp["norm1_g"] = 1.0 + nrm((DEPTH, D), 0.02)
    inp["norm2_g"] = 1.0 + nrm((DEPTH, D), 0.02)
    inp["final_g"] = 1.0 + nrm((D,), 0.02)
    R = N_RWKV
    inp["rwkv_mu"] = unif((R, N_LERP, D), 0.0, 1.0)
    inp["rwkv_w_rkv"] = nrm((R, 3, D, D), D ** -0.5)
    inp["rwkv_w0"] = w0_base[None] + nrm((R, D), 0.1)
    inp["rwkv_w1"] = nrm((R, D, D_DECAY_LORA), D ** -0.5)
    inp["rwkv_w2"] = nrm((R, D_DECAY_LORA, D), 0.1 * D_DECAY_LORA ** -0.5)
    inp["rwkv_a0"] = nrm((R, D), 0.1)
    inp["rwkv_a1"] = nrm((R, D, D_AAA_LORA), D ** -0.5)
    inp["rwkv_a2"] = nrm((R, D_AAA_LORA, D), 0.1 * D_AAA_LORA ** -0.5)
    inp["rwkv_g1"] = nrm((R, D, D_GATE_LORA), D ** -0.5)
    inp["rwkv_g2"] = nrm((R, D_GATE_LORA, D), D_GATE_LORA ** -0.5)
    inp["rwkv_k_k"] = 0.85 + nrm((R, D), 0.02)
    inp["rwkv_k_a"] = 1.0 + nrm((R, D), 0.02)
    inp["rwkv_r_k"] = -0.04 + nrm((R, RWKV_HEADS, RWKV_HEAD), 0.02)
    inp["rwkv_ln_g"] = 1.0 + nrm((R, D), 0.02)
    inp["rwkv_ln_b"] = nrm((R, D), 0.02)
    inp["rwkv_w_o"] = nrm((R, D, D), D ** -0.5)
    Cn = N_CONV
    inp["conv_w_pw1"] = nrm((Cn, D, 2 * D), D ** -0.5)
    inp["conv_b_pw1"] = nrm((Cn, 2 * D), 0.02)
    inp["conv_w_dw"] = nrm((Cn, CONV_WIDTH, D), CONV_WIDTH ** -0.5)
    inp["conv_b_dw"] = nrm((Cn, D), 0.02)
    inp["conv_ln_g"] = 1.0 + nrm((Cn, D), 0.02)
    inp["conv_ln_b"] = nrm((Cn, D), 0.02)
    inp["conv_w_pw2"] = nrm((Cn, D, D), D ** -0.5)
    inp["conv_b_pw2"] = nrm((Cn, D), 0.02)
    inp["moe_w_group"] = nrm((DEPTH, D, N_GROUPS), D ** -0.5)
    inp["moe_b_group"] = nrm((DEPTH, N_GROUPS), 0.01)
    inp["moe_w_expert"] = nrm((DEPTH, D, N_EXPERTS), D ** -0.5)
    inp["moe_b_expert"] = nrm((DEPTH, N_EXPERTS), 0.01)
    inp["moe_w_gate"] = nrm((DEPTH, N_EXPERTS, D, D_EXPERT), D ** -0.5)
    inp["moe_w_up"] = nrm((DEPTH, N_EXPERTS, D, D_EXPERT), D ** -0.5)
    inp["moe_w_down"] = nrm((DEPTH, N_EXPERTS, D_EXPERT, D), D_EXPERT ** -0.5)
    return inp


def reference(x, norm1_g, norm2_g, final_g,
              rwkv_mu, rwkv_w_rkv, rwkv_w0, rwkv_w1, rwkv_w2, rwkv_a0, rwkv_a1, rwkv_a2,
              rwkv_g1, rwkv_g2, rwkv_k_k, rwkv_k_a, rwkv_r_k, rwkv_ln_g, rwkv_ln_b, rwkv_w_o,
              conv_w_pw1, conv_b_pw1, conv_w_dw, conv_b_dw, conv_ln_g, conv_ln_b, conv_w_pw2, conv_b_pw2,
              moe_w_group, moe_b_group, moe_w_expert, moe_b_expert, moe_w_gate, moe_w_up, moe_w_down):
    h = x
    for i in range(DEPTH):
        j = i // N_MIXERS
        hn = rms_norm(h, norm1_g[i])
        if i % N_MIXERS == 0:
            mix = rwkv7_time_mix(hn, rwkv_mu[j], rwkv_w_rkv[j], rwkv_w0[j], rwkv_w1[j], rwkv_w2[j],
                                 rwkv_a0[j], rwkv_a1[j], rwkv_a2[j], rwkv_g1[j], rwkv_g2[j],
                                 rwkv_k_k[j], rwkv_k_a[j], rwkv_r_k[j], rwkv_ln_g[j], rwkv_ln_b[j], rwkv_w_o[j])
        else:
            mix = conformer_conv(hn, conv_w_pw1[j], conv_b_pw1[j], conv_w_dw[j], conv_b_dw[j],
                                 conv_ln_g[j], conv_ln_b[j], conv_w_pw2[j], conv_b_pw2[j])
        h = h + mix
        hn = rms_norm(h, norm2_g[i])
        h = h + hierarchical_moe(hn, moe_w_group[i], moe_b_group[i], moe_w_expert[i], moe_b_expert[i],
                                 moe_w_gate[i], moe_w_up[i], moe_w_down[i])
    return rms_norm(h, final_g)
```

```python
import functools
import math

import jax
import jax.numpy as jnp
from jax import lax
from jax.experimental import pallas as pl
from jax.experimental.pallas import tpu as pltpu

F32 = jnp.float32
BF16 = jnp.bfloat16

HEAD = 64
LNX_EPS = HEAD * 1e-5
CONV_WIDTH = 31
CONV_LN_EPS = 1e-5
N_GROUPS = 4
EXPERTS_PER_GROUP = 8
N_EXPERTS = N_GROUPS * EXPERTS_PER_GROUP
RMS_EPS = 1e-6

LANES = 128
PAIR = 2 * HEAD
CHUNK = 64
VMEM_LIMIT = 56 * 1024 * 1024

NEG_BIG = -1e30


def _dot(a, b):
    return jnp.dot(a, b, preferred_element_type=F32)


def _dot_nt(a, b):
    return lax.dot_general(a, b, (((1,), (1,)), ((), ())), preferred_element_type=F32)


def _dot_tn(a, b):
    return lax.dot_general(a, b, (((0,), (0,)), ((), ())), preferred_element_type=F32)


def _split2(x):
    hi = x.astype(BF16)
    lo = (x - hi.astype(F32)).astype(BF16)
    return hi, lo


def _split3(x):
    hi = x.astype(BF16)
    r1 = x - hi.astype(F32)
    mid = r1.astype(BF16)
    lo = (r1 - mid.astype(F32)).astype(BF16)
    return hi, mid, lo


def _sigmoid(x):
    return 1.0 / (1.0 + jnp.exp(-x))


def _rms(x, g):
    return x * lax.rsqrt(jnp.mean(x * x, axis=-1, keepdims=True) + RMS_EPS) * g


def _rwkv_pre_kernel(x_ref, n1g_ref, mu_ref, wrkv_ref, w0_ref, w1_ref, w2_ref, a0_ref, a1_ref, a2_ref,
                     g1_ref, g2_ref, r_ref, k_ref, v_ref, wl_ref, a_ref, g_ref, prev_ref):
    i = pl.program_id(1)
    tt = x_ref.shape[0]
    hn = _rms(x_ref[...], n1g_ref[...])

    @pl.when(i == 0)
    def _():
        prev_ref[...] = jnp.zeros_like(prev_ref)

    prev = prev_ref[0:1, :]
    rolled = pltpu.roll(hn, 1, 0)
    row = lax.broadcasted_iota(jnp.int32, hn.shape, 0)
    shifted = jnp.where(row == 0, prev, rolled)
    prev_ref[0:1, :] = hn[tt - 1:tt, :]
    xx = shifted - hn

    def mix(j):
        return (hn + xx * mu_ref[j:j + 1, :]).astype(BF16)

    r_ref[...] = _dot(mix(0), wrkv_ref[0])
    k_ref[...] = _dot(mix(2), wrkv_ref[1])
    v_ref[...] = _dot(mix(3), wrkv_ref[2])
    zw = w0_ref[...] + _dot(jnp.tanh(_dot(mix(1), w1_ref[...])).astype(BF16), w2_ref[...])
    wl_ref[...] = -math.exp(-0.5) * _sigmoid(zw)
    a_ref[...] = _sigmoid(a0_ref[...] + _dot(_dot(mix(4), a1_ref[...]).astype(BF16), a2_ref[...]))
    g_ref[...] = _dot(_sigmoid(_dot(mix(5), g1_ref[...])).astype(BF16), g2_ref[...])


def _rwkv_pre(x, n1g, mu, wrkv, w0, w1, w2, a0, a1, a2, g1, g2, *, tt):
    B, T, D = x.shape
    tok = pl.BlockSpec((None, tt, D), lambda b, i: (b, i, 0))

    def full(arr):
        nd = arr.ndim
        return pl.BlockSpec(arr.shape, lambda b, i: (0,) * nd)

    params = (n1g, mu, wrkv, w0, w1, w2, a0, a1, a2, g1, g2)
    out = jax.ShapeDtypeStruct((B, T, D), F32)
    return pl.pallas_call(
        _rwkv_pre_kernel,
        out_shape=(out,) * 6,
        grid=(B, T // tt),
        in_specs=[tok] + [full(p) for p in params],
        out_specs=(tok,) * 6,
        scratch_shapes=[pltpu.VMEM((8, D), F32)],
        compiler_params=pltpu.CompilerParams(
            dimension_semantics=("arbitrary", "arbitrary"), vmem_limit_bytes=VMEM_LIMIT),
        name="rwkv_pre",
    )(x, *params)


def _recurrence_consts():
    L = CHUNK
    lane = lax.broadcasted_iota(jnp.int32, (1, PAIR), 1)
    head0 = (lane < HEAD).astype(F32)
    ri = lax.broadcasted_iota(jnp.int32, (PAIR, PAIR), 0)
    ci = lax.broadcasted_iota(jnp.int32, (PAIR, PAIR), 1)
    bd = ((ri < HEAD) == (ci < HEAD)).astype(BF16)
    eye = (ri == ci).astype(F32)
    ti = lax.broadcasted_iota(jnp.int32, (L, L), 0)
    tj = lax.broadcasted_iota(jnp.int32, (L, L), 1)
    tri = (tj <= ti).astype(BF16)
    mi = lax.broadcasted_iota(jnp.int32, (3 * L, 4 * L), 0)
    mj = lax.broadcasted_iota(jnp.int32, (3 * L, 4 * L), 1)
    t_row = jnp.bitwise_and(mi, L - 1)
    t_col = jnp.bitwise_and(mj, L - 1)
    score_mask = ((t_col < t_row) | ((mi >= 2 * L) & (t_col == t_row))).astype(F32)
    return head0, bd, eye, tri, score_mask


def _chunk_pair(r, k, v, wl, a, k_k, k_a, r_k, ln_g, ln_b, z, consts):
    head0_f, bd, eye, tri, score_mask_f = consts
    head0 = head0_f > 0.5
    score_mask = score_mask_f > 0.5
    L = CHUNK

    def bdsum(x):
        hi, lo = _split2(x)
        return _dot(hi, bd) + _dot(lo, bd)

    def stack(x):
        return jnp.concatenate([jnp.where(head0, x, 0.0), jnp.where(head0, 0.0, x)], axis=0)

    kk = k * k_k
    kk = kk * lax.rsqrt(jnp.maximum(bdsum(kk * kk), 1e-24))
    k2 = k * (1.0 + (a - 1.0) * k_a)
    bb = kk * a
    aa = -kk

    w_hi, w_mid, w_lo = _split3(wl)
    cw = _dot(tri, w_hi) + _dot(tri, w_mid) + _dot(tri, w_lo)
    tot = cw[L - 1:L, :]
    rt = r * jnp.exp(cw)
    at = aa * jnp.exp(cw - wl)
    g_inv = jnp.exp(-cw)
    kt = k2 * g_inv
    bt = bb * g_inv
    g_rem = jnp.exp(tot - cw)
    kh = k2 * g_rem
    bh = bb * g_rem

    a_s = stack(at).astype(BF16)
    v_s = stack(v).astype(BF16)
    lhs = jnp.concatenate([a_s, rt.astype(BF16)], axis=0)
    rhs = jnp.concatenate([stack(kt), stack(bt)], axis=0).astype(BF16)
    scores = jnp.where(score_mask, _dot_nt(lhs, rhs), 0.0)
    a_ak = scores[:2 * L, :2 * L]
    a_ab = scores[:2 * L, 2 * L:]
    a_rp = scores[2 * L:, :]

    inv = eye + a_ab
    m = a_ab.astype(BF16)
    for _ in range(int(math.log2(L)) - 1):
        m2 = _dot(m, m)
        m = m2.astype(BF16)
        inv = inv + _dot(inv.astype(BF16), m)

    zb = z.astype(BF16)
    u_rhs = _dot(a_s, zb) + _dot(a_ak.astype(BF16), v_s)
    u_s = _dot(inv.astype(BF16), u_rhs.astype(BF16)).astype(BF16)
    vu = jnp.concatenate([v_s, u_s], axis=0)
    y = _dot(rt.astype(BF16), zb) + _dot(a_rp.astype(BF16), vu)

    khbh = jnp.concatenate([stack(kh), stack(bh)], axis=0).astype(BF16)
    tot_col = jnp.transpose(jnp.broadcast_to(jnp.exp(tot), (PAIR, PAIR)))
    z_new = z * tot_col + _dot_tn(khbh, vu)

    mean = bdsum(y) * (1.0 / HEAD)
    yc = y - mean
    var = bdsum(yc * yc) * (1.0 / HEAD)
    out = yc * lax.rsqrt(var + LNX_EPS) * ln_g + ln_b + bdsum(r * k2 * r_k) * v
    return out, z_new


def _recurrence_kernel(r_ref, k_ref, v_ref, wl_ref, a_ref, kk_ref, ka_ref, rk_ref, lg_ref, lb_ref,
                       y_ref, z_ref):
    t = pl.program_id(2)
    tb, width = r_ref.shape
    n_pairs = width // PAIR
    consts = _recurrence_consts()

    @pl.when(t == 0)
    def _():
        z_ref[...] = jnp.zeros_like(z_ref)

    def chunk_body(c, carry):
        rows = pl.ds(pl.multiple_of(c * CHUNK, CHUNK), CHUNK)
        for p in range(n_pairs):
            cols = slice(p * PAIR, (p + 1) * PAIR)
            out, z_new = _chunk_pair(
                r_ref[rows, cols], k_ref[rows, cols], v_ref[rows, cols], wl_ref[rows, cols], a_ref[rows, cols],
                kk_ref[:, cols], ka_ref[:, cols], rk_ref[:, cols], lg_ref[:, cols], lb_ref[:, cols],
                z_ref[p], consts)
            y_ref[rows, cols] = out
            z_ref[p] = z_new
        return carry

    lax.fori_loop(0, tb // CHUNK, chunk_body, 0)


def _recurrence(r, k, v, wl, a, k_k, k_a, r_k, ln_g, ln_b, *, tb, width):
    B, T, D = r.shape
    tok = pl.BlockSpec((None, tb, width), lambda b, g, t: (b, t, g))
    par = pl.BlockSpec((1, width), lambda b, g, t: (0, g))
    return pl.pallas_call(
        _recurrence_kernel,
        out_shape=jax.ShapeDtypeStruct((B, T, D), F32),
        grid=(B, D // width, T // tb),
        in_specs=[tok] * 5 + [par] * 5,
        out_specs=tok,
        scratch_shapes=[pltpu.VMEM((width // PAIR, PAIR, PAIR), F32)],
        compiler_params=pltpu.CompilerParams(
            dimension_semantics=("arbitrary", "arbitrary", "arbitrary"), vmem_limit_bytes=VMEM_LIMIT),
        name="rwkv_recurrence",
    )(r, k, v, wl, a, k_k, k_a, r_k, ln_g, ln_b)


def _route(h1, n2g, wr_hi, wr_lo, br):
    hn = _rms(h1, n2g)
    x_hi, x_lo = _split2(hn)
    logits = _dot(x_hi, wr_hi) + _dot(x_lo, wr_hi) + _dot(x_hi, wr_lo) + br
    lane = lax.broadcasted_iota(jnp.int32, logits.shape, 1)

    def first_argmax(vals):
        mx = jnp.max(vals, axis=-1, keepdims=True)
        idx = jnp.min(jnp.where(vals == mx, lane, 4 * LANES), axis=-1, keepdims=True)
        return mx, idx

    lg = jnp.where((lane >= N_EXPERTS) & (lane < N_EXPERTS + N_GROUPS), logits, NEG_BIG)
    g_max, g_lane = first_argmax(lg)
    p_top = 1.0 / jnp.sum(jnp.exp(lg - g_max), axis=-1, keepdims=True)
    g_idx = g_lane - N_EXPERTS
    in_group = (lane >= g_idx * EXPERTS_PER_GROUP) & (lane < (g_idx + 1) * EXPERTS_PER_GROUP)
    le = jnp.where(in_group, logits, NEG_BIG)
    m1, i1 = first_argmax(le)
    m2, i2 = first_argmax(jnp.where(lane == i1, NEG_BIG, le))
    e2 = jnp.exp(m2 - m1)
    w1 = p_top / (1.0 + e2)
    w2 = p_top * e2 / (1.0 + e2)
    gates = jnp.where(lane == i1, w1, jnp.where(lane == i2, w2, 0.0))
    return hn.astype(BF16), gates


def _rwkv_post_kernel(y_ref, g_ref, x_ref, wo_ref, n2g_ref, wrh_ref, wrl_ref, br_ref,
                      h1_ref, hn_ref, gates_ref):
    yg = (y_ref[...] * g_ref[...]).astype(BF16)
    h1 = x_ref[...] + _dot(yg, wo_ref[...])
    h1_ref[...] = h1
    hn, gates = _route(h1, n2g_ref[...], wrh_ref[...], wrl_ref[...], br_ref[...])
    hn_ref[...] = hn
    gates_ref[...] = gates


def _rwkv_post(y, g, x, wo, n2g, wr_hi, wr_lo, br, *, tt):
    N, D = x.shape
    tok = pl.BlockSpec((tt, D), lambda i: (i, 0))

    def full(arr):
        nd = arr.ndim
        return pl.BlockSpec(arr.shape, lambda i: (0,) * nd)

    params = (wo, n2g, wr_hi, wr_lo, br)
    return pl.pallas_call(
        _rwkv_post_kernel,
        out_shape=(jax.ShapeDtypeStruct((N, D), F32), jax.ShapeDtypeStruct((N, D), BF16),
                   jax.ShapeDtypeStruct((N, LANES), F32)),
        grid=(N // tt,),
        in_specs=[tok, tok, tok] + [full(p) for p in params],
        out_specs=(tok, tok, pl.BlockSpec((tt, LANES), lambda i: (i, 0))),
        compiler_params=pltpu.CompilerParams(
            dimension_semantics=("arbitrary",), vmem_limit_bytes=VMEM_LIMIT),
        name="rwkv_post",
    )(y, g, x, *params)


HALO = 32


def _conformer_kernel(x_ref, n1g_ref, wpw1_ref, bpw1_ref, wdw_ref, bdw_ref, lng_ref, lnb_ref, wpw2_ref, bpw2_ref,
                      n2g_ref, wrh_ref, wrl_ref, br_ref, h1_ref, hn_ref, gates_ref, ubuf_ref):
    i = pl.program_id(1)
    tt, D = x_ref.shape
    x = x_ref[...]
    hn = _rms(x, n1g_ref[...]).astype(BF16)
    u = _dot(hn, wpw1_ref[...]) + bpw1_ref[...]
    u = u[:, :D] * _sigmoid(u[:, D:])

    @pl.when(i == 0)
    def _():
        ubuf_ref[0:HALO, :] = jnp.zeros((HALO, D), F32)

    ubuf_ref[HALO:HALO + tt, :] = u
    base = HALO - (CONV_WIDTH - 1)
    acc = jnp.zeros((tt, D), F32) + bdw_ref[...]
    for j in range(CONV_WIDTH):
        acc = acc + wdw_ref[j:j + 1, :] * ubuf_ref[base + j:base + j + tt, :]
    ubuf_ref[0:HALO, :] = ubuf_ref[tt:tt + HALO, :]

    m = jnp.mean(acc, axis=-1, keepdims=True)
    c = acc - m
    var = jnp.mean(c * c, axis=-1, keepdims=True)
    c = c * lax.rsqrt(var + CONV_LN_EPS) * lng_ref[...] + lnb_ref[...]
    c = c * _sigmoid(c)
    h1 = x + _dot(c.astype(BF16), wpw2_ref[...]) + bpw2_ref[...]
    h1_ref[...] = h1
    hn2, gates = _route(h1, n2g_ref[...], wrh_ref[...], wrl_ref[...], br_ref[...])
    hn_ref[...] = hn2
    gates_ref[...] = gates


def _conformer(x, n1g, wpw1, bpw1, wdw, bdw, lng, lnb, wpw2, bpw2, n2g, wr_hi, wr_lo, br, *, tt):
    B, T, D = x.shape
    tok = pl.BlockSpec((None, tt, D), lambda b, i: (b, i, 0))

    def full(arr):
        nd = arr.ndim
        return pl.BlockSpec(arr.shape, lambda b, i: (0,) * nd)

    params = (n1g, wpw1, bpw1, wdw, bdw, lng, lnb, wpw2, bpw2, n2g, wr_hi, wr_lo, br)
    return pl.pallas_call(
        _conformer_kernel,
        out_shape=(jax.ShapeDtypeStruct((B, T, D), F32), jax.ShapeDtypeStruct((B, T, D), BF16),
                   jax.ShapeDtypeStruct((B, T, LANES), F32)),
        grid=(B, T // tt),
        in_specs=[tok] + [full(p) for p in params],
        out_specs=(tok, tok, pl.BlockSpec((None, tt, LANES), lambda b, i: (b, i, 0))),
        scratch_shapes=[pltpu.VMEM((HALO + tt, D), F32)],
        compiler_params=pltpu.CompilerParams(
            dimension_semantics=("arbitrary", "arbitrary"), vmem_limit_bytes=VMEM_LIMIT),
        name="conformer",
    )(x, *params)


def _moe_kernel(hn_ref, gates_ref, h1_ref, wg_ref, wu_ref, wd_ref, fg_ref, out_ref, *, eb, final_norm):
    e = pl.program_id(1)
    n_e = pl.num_programs(1)
    x = hn_ref[...]
    gates = gates_ref[...]
    lane = lax.broadcasted_iota(jnp.int32, gates.shape, 1)

    @pl.when(e == 0)
    def _():
        out_ref[...] = h1_ref[...]

    zs = []
    for q in range(eb):
        ge = jnp.sum(jnp.where(lane == e * eb + q, gates, 0.0), axis=-1, keepdims=True)
        zg = _dot(x, wg_ref[q])
        zu = _dot(x, wu_ref[q])
        zs.append((zg * _sigmoid(zg) * zu * ge).astype(BF16))
    z = jnp.concatenate(zs, axis=-1)
    wd = wd_ref[...].reshape(-1, wd_ref.shape[-1])
    out_ref[...] += _dot(z, wd)

    if final_norm:
        @pl.when(e == n_e - 1)
        def _():
            out_ref[...] = _rms(out_ref[...], fg_ref[...])


def _moe(hn, gates, h1, wg, wu, wd, fg, *, tm, eb, final_norm):
    N, D = h1.shape
    E, _, F = wg.shape
    return pl.pallas_call(
        functools.partial(_moe_kernel, eb=eb, final_norm=final_norm),
        out_shape=jax.ShapeDtypeStruct((N, D), F32),
        grid=(N // tm, E // eb),
        in_specs=[pl.BlockSpec((tm, D), lambda i, e: (i, 0)),
                  pl.BlockSpec((tm, LANES), lambda i, e: (i, 0)),
                  pl.BlockSpec((tm, D), lambda i, e: (i, 0)),
                  pl.BlockSpec((eb, D, F), lambda i, e: (e, 0, 0)),
                  pl.BlockSpec((eb, D, F), lambda i, e: (e, 0, 0)),
                  pl.BlockSpec((eb, F, D), lambda i, e: (e, 0, 0)),
                  pl.BlockSpec((1, D), lambda i, e: (0, 0))],
        out_specs=pl.BlockSpec((tm, D), lambda i, e: (i, 0)),
        compiler_params=pltpu.CompilerParams(
            dimension_semantics=("arbitrary", "arbitrary"), vmem_limit_bytes=VMEM_LIMIT),
        name="moe_final" if final_norm else "moe",
    )(hn, gates, h1, wg, wu, wd, fg)


def _router_weights(w_group, b_group, w_expert, b_expert):
    D = w_group.shape[0]
    w = jnp.zeros((D, LANES), F32).at[:, :N_EXPERTS].set(w_expert).at[:, N_EXPERTS:N_EXPERTS + N_GROUPS].set(w_group)
    b = jnp.zeros((1, LANES), F32).at[0, :N_EXPERTS].set(b_expert).at[0, N_EXPERTS:N_EXPERTS + N_GROUPS].set(b_group)
    hi = w.astype(BF16)
    lo = (w - hi.astype(F32)).astype(BF16)
    return hi, lo, b


def _pad_cols(w, n):
    return jnp.pad(w, ((0, 0), (0, n - w.shape[1])))


def _pad_rows(w, n):
    return jnp.pad(w, ((0, n - w.shape[0]), (0, 0)))


def _tile(n, want):
    t = min(n, want)
    while n % t:
        t //= 2
    return t


def kernel(x, norm1_g, norm2_g, final_g, rwkv_mu, rwkv_w_rkv, rwkv_w0, rwkv_w1, rwkv_w2, rwkv_a0, rwkv_a1, rwkv_a2, rwkv_g1, rwkv_g2, rwkv_k_k, rwkv_k_a, rwkv_r_k, rwkv_ln_g, rwkv_ln_b, rwkv_w_o, conv_w_pw1, conv_b_pw1, conv_w_dw, conv_b_dw, conv_ln_g, conv_ln_b, conv_w_pw2, conv_b_pw2, moe_w_group, moe_b_group, moe_w_expert, moe_b_expert, moe_w_gate, moe_w_up, moe_w_down):
    B, T, D = x.shape
    N = B * T
    depth = norm1_g.shape[0]
    assert depth == 2 and D % PAIR == 0 and T % CHUNK == 0
    row = lambda v: v.reshape(1, -1)
    bf = lambda w: w.astype(BF16)

    r, k, v, wl, a, g = _rwkv_pre(
        x, row(norm1_g[0]), rwkv_mu[0], bf(rwkv_w_rkv[0]), row(rwkv_w0[0]),
        bf(_pad_cols(rwkv_w1[0], LANES)), bf(_pad_rows(rwkv_w2[0], LANES)), row(rwkv_a0[0]),
        bf(_pad_cols(rwkv_a1[0], LANES)), bf(_pad_rows(rwkv_a2[0], LANES)),
        bf(rwkv_g1[0]), bf(rwkv_g2[0]), tt=_tile(T, 256))
    y = _recurrence(r, k, v, wl, a, row(rwkv_k_k[0]), row(rwkv_k_a[0]), rwkv_r_k[0].reshape(1, -1),
                    row(rwkv_ln_g[0]), row(rwkv_ln_b[0]), tb=_tile(T, 512), width=_tile(D, 512))
    wr_hi, wr_lo, br = _router_weights(moe_w_group[0], moe_b_group[0], moe_w_expert[0], moe_b_expert[0])
    h1, hn, gates = _rwkv_post(y.reshape(N, D), g.reshape(N, D), x.reshape(N, D), bf(rwkv_w_o[0]),
                               row(norm2_g[0]), wr_hi, wr_lo, br, tt=_tile(N, 512))
    h = _moe(hn, gates, h1, bf(moe_w_gate[0]), bf(moe_w_up[0]), bf(moe_w_down[0]), row(final_g),
             tm=_tile(N, 1024), eb=4, final_norm=False)

    wr_hi, wr_lo, br = _router_weights(moe_w_group[1], moe_b_group[1], moe_w_expert[1], moe_b_expert[1])
    h1, hn, gates = _conformer(
        h.reshape(B, T, D), row(norm1_g[1]), bf(conv_w_pw1[0]), row(conv_b_pw1[0]), conv_w_dw[0], row(conv_b_dw[0]),
        row(conv_ln_g[0]), row(conv_ln_b[0]), bf(conv_w_pw2[0]), row(conv_b_pw2[0]),
        row(norm2_g[1]), wr_hi, wr_lo, br, tt=_tile(T, 256))
    out = _moe(hn.reshape(N, D), gates.reshape(N, LANES), h1.reshape(N, D),
               bf(moe_w_gate[1]), bf(moe_w_up[1]), bf(moe_w_down[1]), row(final_g),
               tm=_tile(N, 1024), eb=4, final_norm=True)
    return out.reshape(B, T, D)
```

```python
import functools
import math

import jax
import jax.numpy as jnp
from jax import lax
from jax.experimental import pallas as pl
from jax.experimental.pallas import tpu as pltpu

F32 = jnp.float32
BF16 = jnp.bfloat16

HEAD = 64
LNX_EPS = HEAD * 1e-5
CONV_WIDTH = 31
CONV_LN_EPS = 1e-5
N_GROUPS = 4
EXPERTS_PER_GROUP = 8
N_EXPERTS = N_GROUPS * EXPERTS_PER_GROUP
RMS_EPS = 1e-6

LANES = 128
QUAD = 4 * HEAD
CHUNK = 64
VMEM_LIMIT = 56 * 1024 * 1024

NEG_BIG = -1e30


def _dot(a, b):
    return jnp.dot(a, b, preferred_element_type=F32)


def _dot_nt(a, b):
    return lax.dot_general(a, b, (((1,), (1,)), ((), ())), preferred_element_type=F32)


def _dot_tn(a, b):
    return lax.dot_general(a, b, (((0,), (0,)), ((), ())), preferred_element_type=F32)


def _split2(x):
    hi = x.astype(BF16)
    lo = (x - hi.astype(F32)).astype(BF16)
    return hi, lo


def _split3(x):
    hi = x.astype(BF16)
    r1 = x - hi.astype(F32)
    mid = r1.astype(BF16)
    lo = (r1 - mid.astype(F32)).astype(BF16)
    return hi, mid, lo


def _sigmoid(x):
    return 1.0 / (1.0 + jnp.exp(-x))


def _rms(x, g):
    return x * lax.rsqrt(jnp.mean(x * x, axis=-1, keepdims=True) + RMS_EPS) * g


def _rwkv_pre_kernel(x_ref, n1g_ref, mu_ref, wrkv_ref, w0_ref, w1_ref, w2_ref, a0_ref, a1_ref, a2_ref,
                     g1_ref, g2_ref, r_ref, k_ref, v_ref, wl_ref, a_ref, g_ref, prev_ref):
    i = pl.program_id(1)
    tt = x_ref.shape[0]
    hn = _rms(x_ref[...], n1g_ref[...])

    @pl.when(i == 0)
    def _():
        prev_ref[...] = jnp.zeros_like(prev_ref)

    prev = prev_ref[0:1, :]
    rolled = pltpu.roll(hn, 1, 0)
    row = lax.broadcasted_iota(jnp.int32, hn.shape, 0)
    shifted = jnp.where(row == 0, prev, rolled)
    prev_ref[0:1, :] = hn[tt - 1:tt, :]
    xx = shifted - hn

    def mix(j):
        return (hn + xx * mu_ref[j:j + 1, :]).astype(BF16)

    r_ref[...] = _dot(mix(0), wrkv_ref[0])
    k_ref[...] = _dot(mix(2), wrkv_ref[1])
    v_ref[...] = _dot(mix(3), wrkv_ref[2])
    zw = w0_ref[...] + _dot(jnp.tanh(_dot(mix(1), w1_ref[...])).astype(BF16), w2_ref[...])
    wl_ref[...] = -math.exp(-0.5) * _sigmoid(zw)
    a_ref[...] = _sigmoid(a0_ref[...] + _dot(_dot(mix(4), a1_ref[...]).astype(BF16), a2_ref[...]))
    g_ref[...] = _dot(_sigmoid(_dot(mix(5), g1_ref[...])).astype(BF16), g2_ref[...])


def _rwkv_pre(x, n1g, mu, wrkv, w0, w1, w2, a0, a1, a2, g1, g2, *, tt):
    B, T, D = x.shape
    tok = pl.BlockSpec((None, tt, D), lambda b, i: (b, i, 0))

    def full(arr):
        nd = arr.ndim
        return pl.BlockSpec(arr.shape, lambda b, i: (0,) * nd)

    params = (n1g, mu, wrkv, w0, w1, w2, a0, a1, a2, g1, g2)
    out = jax.ShapeDtypeStruct((B, T, D), F32)
    return pl.pallas_call(
        _rwkv_pre_kernel,
        out_shape=(out,) * 6,
        grid=(B, T // tt),
        in_specs=[tok] + [full(p) for p in params],
        out_specs=(tok,) * 6,
        scratch_shapes=[pltpu.VMEM((8, D), F32)],
        compiler_params=pltpu.CompilerParams(
            dimension_semantics=("arbitrary", "arbitrary"), vmem_limit_bytes=VMEM_LIMIT),
        name="rwkv_pre",
    )(x, *params)


def _rec_consts(tb):
    L, Q = CHUNK, QUAD
    ri = lax.broadcasted_iota(jnp.int32, (Q, Q), 0)
    ci = lax.broadcasted_iota(jnp.int32, (Q, Q), 1)
    bd = (jnp.right_shift(ri, 6) == jnp.right_shift(ci, 6)).astype(F32)
    ei = lax.broadcasted_iota(jnp.int32, (L, 4 * L), 0)
    ej = lax.broadcasted_iota(jnp.int32, (L, 4 * L), 1)
    eye_lp = (jnp.bitwise_and(ej, L - 1) == ei).astype(F32)
    mi = lax.broadcasted_iota(jnp.int32, (2 * L, 8 * L), 0)
    mj = lax.broadcasted_iota(jnp.int32, (2 * L, 8 * L), 1)
    t_row = jnp.bitwise_and(mi, L - 1)
    t_col = jnp.bitwise_and(mj, L - 1)
    score_mask = ((t_col < t_row) | ((mi >= L) & (t_col == t_row))).astype(F32)
    bi = lax.broadcasted_iota(jnp.int32, (tb, tb), 0)
    bj = lax.broadcasted_iota(jnp.int32, (tb, tb), 1)
    tri_bd = ((jnp.right_shift(bi, 6) == jnp.right_shift(bj, 6)) & (bj <= bi)).astype(BF16)
    return bd, eye_lp, score_mask, tri_bd


def _recurrence_kernel(r_ref, k_ref, v_ref, wl_ref, a_ref, kk_ref, ka_ref, rk_ref, lg_ref, lb_ref,
                       y_ref, z_ref, rt_s, at_s, kt_s, bt_s, kh_s, bh_s, vb_s, w_s, arb_s, u0_s, y0_s,
                       kbt_s, gcol_s):
    assert CHUNK == HEAD and QUAD == 4 * CHUNK
    L, Q = CHUNK, QUAD
    t = pl.program_id(2)
    tb, width = r_ref.shape
    nq, nc = width // Q, tb // L
    bd_f, eye_lp, score_mask_f, tri_bd = _rec_consts(tb)
    bd_b = bd_f.astype(BF16)

    def bdsum(x):
        hi, lo = _split2(x)
        return _dot(hi, bd_b) + _dot(lo, bd_b)

    def stack(xb):
        return jnp.concatenate([xb] * 4, axis=0) * bd_b

    @pl.when(t == 0)
    def _():
        z_ref[...] = jnp.zeros_like(z_ref)

    for q in range(nq):
        cols = slice(q * Q, (q + 1) * Q)
        r, k, v, wl, a = r_ref[:, cols], k_ref[:, cols], v_ref[:, cols], wl_ref[:, cols], a_ref[:, cols]
        kk = k * kk_ref[:, cols]
        kk = kk * lax.rsqrt(jnp.maximum(bdsum(kk * kk), 1e-24))
        k2 = k * (1.0 + (a - 1.0) * ka_ref[:, cols])
        bb = kk * a
        y_ref[:, cols] = bdsum(r * k2 * rk_ref[:, cols]) * v
        w_hi, w_mid, w_lo = _split3(wl)
        cw = _dot(tri_bd, w_hi) + _dot(tri_bd, w_mid) + _dot(tri_bd, w_lo)
        rt_s[:, cols] = (r * jnp.exp(cw)).astype(BF16)
        at_s[:, cols] = (-kk * jnp.exp(cw - wl)).astype(BF16)
        g_inv = jnp.exp(-cw)
        kt_s[:, cols] = (k2 * g_inv).astype(BF16)
        bt_s[:, cols] = (bb * g_inv).astype(BF16)
        vb_s[:, cols] = v.astype(BF16)
        for c in range(nc):
            rows = slice(c * L, (c + 1) * L)
            tot = cw[(c + 1) * L - 1:(c + 1) * L, :]
            g_rem = jnp.exp(tot - cw[rows, :])
            kh_s[rows, cols] = (k2[rows, :] * g_rem).astype(BF16)
            bh_s[rows, cols] = (bb[rows, :] * g_rem).astype(BF16)
            gcol_s[c * nq + q] = jnp.broadcast_to(jnp.transpose(jnp.exp(tot)), (Q, LANES))

    tiles = [(slice(c * L, (c + 1) * L), slice(q * Q, (q + 1) * Q), c * nq + q)
             for c in range(nc) for q in range(nq)]
    score_mask = score_mask_f > 0.5
    for rows, cols, j in tiles:
        rhs = jnp.concatenate([stack(kt_s[rows, cols]), stack(bt_s[rows, cols])], axis=0)
        lhs = jnp.concatenate([at_s[rows, cols], rt_s[rows, cols]], axis=0)
        sc = jnp.where(score_mask, _dot_nt(lhs, rhs), 0.0)
        kt_s[rows, cols] = sc[:L, :4 * L].astype(BF16)
        bt_s[rows, cols] = sc[L:, :4 * L].astype(BF16)
        arb_s[rows, cols] = sc[L:, 4 * L:].astype(BF16)
        a_ab = sc[:L, 4 * L:]
        w_s[rows, cols] = a_ab.astype(BF16)
        u0_s[rows, cols] = eye_lp + a_ab
        khbh = jnp.concatenate([kh_s[rows, cols], bh_s[rows, cols]], axis=0).astype(F32)
        kbt_s[j] = jnp.transpose(khbh).astype(BF16)
    for rows, cols, j in tiles:
        nb = w_s[rows, cols]
        w_s[rows, cols] = _dot(nb, stack(nb)).astype(BF16)
    for _ in range(int(math.log2(L)) - 2):
        for rows, cols, j in tiles:
            mb = w_s[rows, cols]
            tinv = u0_s[rows, cols]
            res = _dot(jnp.concatenate([mb, tinv.astype(BF16)], axis=0), stack(mb))
            w_s[rows, cols] = res[:L].astype(BF16)
            u0_s[rows, cols] = tinv + res[L:]
    for rows, cols, j in tiles:
        lhs = jnp.concatenate([kt_s[rows, cols], bt_s[rows, cols]], axis=0)
        pu_y0 = _dot(lhs, stack(vb_s[rows, cols]))
        kt_s[rows, cols] = pu_y0[:L].astype(BF16)
        y0_s[rows, cols] = pu_y0[L:]
    for rows, cols, j in tiles:
        tinv = u0_s[rows, cols]
        tinv = tinv + _dot(tinv.astype(BF16), stack(w_s[rows, cols]))
        wu = _dot(tinv.astype(BF16),
                  jnp.concatenate([stack(at_s[rows, cols]), stack(kt_s[rows, cols])], axis=1))
        w_s[rows, cols] = wu[:, :Q].astype(BF16)
        u0_s[rows, cols] = wu[:, Q:]

    def phase2(c, carry):
        rows = pl.ds(pl.multiple_of(c * L, L), L)
        for q in range(nq):
            cols = slice(q * Q, (q + 1) * Q)
            z = z_ref[q]
            wr = _dot(jnp.concatenate([w_s[rows, cols], rt_s[rows, cols]], axis=0), z.astype(BF16))
            usb = (wr[:L] + u0_s[rows, cols]).astype(BF16)
            y0_s[rows, cols] = y0_s[rows, cols] + wr[L:] + _dot(arb_s[rows, cols], stack(usb))
            upd = _dot(kbt_s[c * nq + q], jnp.concatenate([vb_s[rows, cols], usb], axis=0))
            g = gcol_s[c * nq + q]
            z_ref[q] = z * jnp.concatenate([g, g], axis=1) + upd * bd_f
        return carry

    lax.fori_loop(0, nc, phase2, 0)

    for q in range(nq):
        cols = slice(q * Q, (q + 1) * Q)
        y = y0_s[:, cols]
        yc = y - bdsum(y) * (1.0 / HEAD)
        var = bdsum(yc * yc) * (1.0 / HEAD)
        y_ref[:, cols] = y_ref[:, cols] + yc * lax.rsqrt(var + LNX_EPS) * lg_ref[:, cols] + lb_ref[:, cols]


def _recurrence(r, k, v, wl, a, k_k, k_a, r_k, ln_g, ln_b, *, tb, width):
    B, T, D = r.shape
    tok = pl.BlockSpec((None, tb, width), lambda b, g, t: (b, t, g))
    par = pl.BlockSpec((1, width), lambda b, g, t: (0, g))
    n_qc = (width // QUAD) * (tb // CHUNK)
    blk_bf = pltpu.VMEM((tb, width), BF16)
    blk_f32 = pltpu.VMEM((tb, width), F32)
    return pl.pallas_call(
        _recurrence_kernel,
        out_shape=jax.ShapeDtypeStruct((B, T, D), F32),
        grid=(B, D // width, T // tb),
        in_specs=[tok] * 5 + [par] * 5,
        out_specs=tok,
        scratch_shapes=[pltpu.VMEM((width // QUAD, QUAD, QUAD), F32)] + [blk_bf] * 9 + [blk_f32] * 2
        + [pltpu.VMEM((n_qc, QUAD, 2 * CHUNK), BF16), pltpu.VMEM((n_qc, QUAD, LANES), F32)],
        compiler_params=pltpu.CompilerParams(
            dimension_semantics=("arbitrary", "arbitrary", "arbitrary"), vmem_limit_bytes=VMEM_LIMIT),
        name="rwkv_recurrence",
    )(r, k, v, wl, a, k_k, k_a, r_k, ln_g, ln_b)


def _route(h1, n2g, wr_hi, wr_lo, br):
    hn = _rms(h1, n2g)
    x_hi, x_lo = _split2(hn)
    logits = _dot(x_hi, wr_hi) + _dot(x_lo, wr_hi) + _dot(x_hi, wr_lo) + br
    lane = lax.broadcasted_iota(jnp.int32, logits.shape, 1)

    def first_argmax(vals):
        mx = jnp.max(vals, axis=-1, keepdims=True)
        idx = jnp.min(jnp.where(vals == mx, lane, 4 * LANES), axis=-1, keepdims=True)
        return mx, idx

    lg = jnp.where((lane >= N_EXPERTS) & (lane < N_EXPERTS + N_GROUPS), logits, NEG_BIG)
    g_max, g_lane = first_argmax(lg)
    p_top = 1.0 / jnp.sum(jnp.exp(lg - g_max), axis=-1, keepdims=True)
    g_idx = g_lane - N_EXPERTS
    in_group = (lane >= g_idx * EXPERTS_PER_GROUP) & (lane < (g_idx + 1) * EXPERTS_PER_GROUP)
    le = jnp.where(in_group, logits, NEG_BIG)
    m1, i1 = first_argmax(le)
    m2, i2 = first_argmax(jnp.where(lane == i1, NEG_BIG, le))
    e2 = jnp.exp(m2 - m1)
    w1 = p_top / (1.0 + e2)
    w2 = p_top * e2 / (1.0 + e2)
    gates = jnp.where(lane == i1, w1, jnp.where(lane == i2, w2, 0.0))
    return hn.astype(BF16), gates


def _rwkv_post_kernel(y_ref, g_ref, x_ref, wo_ref, n2g_ref, wrh_ref, wrl_ref, br_ref,
                      h1_ref, hn_ref, gates_ref):
    yg = (y_ref[...] * g_ref[...]).astype(BF16)
    h1 = x_ref[...] + _dot(yg, wo_ref[...])
    h1_ref[...] = h1
    hn, gates = _route(h1, n2g_ref[...], wrh_ref[...], wrl_ref[...], br_ref[...])
    hn_ref[...] = hn
    gates_ref[...] = gates


def _rwkv_post(y, g, x, wo, n2g, wr_hi, wr_lo, br, *, tt):
    N, D = x.shape
    tok = pl.BlockSpec((tt, D), lambda i: (i, 0))

    def full(arr):
        nd = arr.ndim
        return pl.BlockSpec(arr.shape, lambda i: (0,) * nd)

    params = (wo, n2g, wr_hi, wr_lo, br)
    return pl.pallas_call(
        _rwkv_post_kernel,
        out_shape=(jax.ShapeDtypeStruct((N, D), F32), jax.ShapeDtypeStruct((N, D), BF16),
                   jax.ShapeDtypeStruct((N, LANES), F32)),
        grid=(N // tt,),
        in_specs=[tok, tok, tok] + [full(p) for p in params],
        out_specs=(tok, tok, pl.BlockSpec((tt, LANES), lambda i: (i, 0))),
        compiler_params=pltpu.CompilerParams(
            dimension_semantics=("arbitrary",), vmem_limit_bytes=VMEM_LIMIT),
        name="rwkv_post",
    )(y, g, x, *params)


HALO = 32


def _conformer_kernel(x_ref, n1g_ref, wpw1_ref, bpw1_ref, wdw_ref, bdw_ref, lng_ref, lnb_ref, wpw2_ref, bpw2_ref,
                      n2g_ref, wrh_ref, wrl_ref, br_ref, h1_ref, hn_ref, gates_ref, ubuf_ref):
    i = pl.program_id(1)
    tt, D = x_ref.shape
    x = x_ref[...]
    hn = _rms(x, n1g_ref[...]).astype(BF16)
    u = _dot(hn, wpw1_ref[...]) + bpw1_ref[...]
    u = u[:, :D] * _sigmoid(u[:, D:])

    @pl.when(i == 0)
    def _():
        ubuf_ref[0:HALO, :] = jnp.zeros((HALO, D), F32)

    ubuf_ref[HALO:HALO + tt, :] = u
    base = HALO - (CONV_WIDTH - 1)
    acc = jnp.zeros((tt, D), F32) + bdw_ref[...]
    for j in range(CONV_WIDTH):
        acc = acc + wdw_ref[j:j + 1, :] * ubuf_ref[base + j:base + j + tt, :]
    ubuf_ref[0:HALO, :] = ubuf_ref[tt:tt + HALO, :]

    m = jnp.mean(acc, axis=-1, keepdims=True)
    c = acc - m
    var = jnp.mean(c * c, axis=-1, keepdims=True)
    c = c * lax.rsqrt(var + CONV_LN_EPS) * lng_ref[...] + lnb_ref[...]
    c = c * _sigmoid(c)
    h1 = x + _dot(c.astype(BF16), wpw2_ref[...]) + bpw2_ref[...]
    h1_ref[...] = h1
    hn2, gates = _route(h1, n2g_ref[...], wrh_ref[...], wrl_ref[...], br_ref[...])
    hn_ref[...] = hn2
    gates_ref[...] = gates


def _conformer(x, n1g, wpw1, bpw1, wdw, bdw, lng, lnb, wpw2, bpw2, n2g, wr_hi, wr_lo, br, *, tt):
    B, T, D = x.shape
    tok = pl.BlockSpec((None, tt, D), lambda b, i: (b, i, 0))

    def full(arr):
        nd = arr.ndim
        return pl.BlockSpec(arr.shape, lambda b, i: (0,) * nd)

    params = (n1g, wpw1, bpw1, wdw, bdw, lng, lnb, wpw2, bpw2, n2g, wr_hi, wr_lo, br)
    return pl.pallas_call(
        _conformer_kernel,
        out_shape=(jax.ShapeDtypeStruct((B, T, D), F32), jax.ShapeDtypeStruct((B, T, D), BF16),
                   jax.ShapeDtypeStruct((B, T, LANES), F32)),
        grid=(B, T // tt),
        in_specs=[tok] + [full(p) for p in params],
        out_specs=(tok, tok, pl.BlockSpec((None, tt, LANES), lambda b, i: (b, i, 0))),
        scratch_shapes=[pltpu.VMEM((HALO + tt, D), F32)],
        compiler_params=pltpu.CompilerParams(
            dimension_semantics=("arbitrary", "arbitrary"), vmem_limit_bytes=VMEM_LIMIT),
        name="conformer",
    )(x, *params)


def _moe_kernel(hn_ref, gates_ref, h1_ref, wg_ref, wu_ref, wd_ref, fg_ref, out_ref, *, eb, final_norm):
    e = pl.program_id(1)
    n_e = pl.num_programs(1)
    x = hn_ref[...]
    gates = gates_ref[...]
    lane = lax.broadcasted_iota(jnp.int32, gates.shape, 1)

    @pl.when(e == 0)
    def _():
        out_ref[...] = h1_ref[...]

    zs = []
    for q in range(eb):
        ge = jnp.sum(jnp.where(lane == e * eb + q, gates, 0.0), axis=-1, keepdims=True)
        zg = _dot(x, wg_ref[q])
        zu = _dot(x, wu_ref[q])
        zs.append((zg * _sigmoid(zg) * zu * ge).astype(BF16))
    z = jnp.concatenate(zs, axis=-1)
    wd = wd_ref[...].reshape(-1, wd_ref.shape[-1])
    out_ref[...] += _dot(z, wd)

    if final_norm:
        @pl.when(e == n_e - 1)
        def _():
            out_ref[...] = _rms(out_ref[...], fg_ref[...])


def _moe(hn, gates, h1, wg, wu, wd, fg, *, tm, eb, final_norm):
    N, D = h1.shape
    E, _, F = wg.shape
    return pl.pallas_call(
        functools.partial(_moe_kernel, eb=eb, final_norm=final_norm),
        out_shape=jax.ShapeDtypeStruct((N, D), F32),
        grid=(N // tm, E // eb),
        in_specs=[pl.BlockSpec((tm, D), lambda i, e: (i, 0)),
                  pl.BlockSpec((tm, LANES), lambda i, e: (i, 0)),
                  pl.BlockSpec((tm, D), lambda i, e: (i, 0)),
                  pl.BlockSpec((eb, D, F), lambda i, e: (e, 0, 0)),
                  pl.BlockSpec((eb, D, F), lambda i, e: (e, 0, 0)),
                  pl.BlockSpec((eb, F, D), lambda i, e: (e, 0, 0)),
                  pl.BlockSpec((1, D), lambda i, e: (0, 0))],
        out_specs=pl.BlockSpec((tm, D), lambda i, e: (i, 0)),
        compiler_params=pltpu.CompilerParams(
            dimension_semantics=("arbitrary", "arbitrary"), vmem_limit_bytes=VMEM_LIMIT),
        name="moe_final" if final_norm else "moe",
    )(hn, gates, h1, wg, wu, wd, fg)


def _router_weights(w_group, b_group, w_expert, b_expert):
    D = w_group.shape[0]
    w = jnp.zeros((D, LANES), F32).at[:, :N_EXPERTS].set(w_expert).at[:, N_EXPERTS:N_EXPERTS + N_GROUPS].set(w_group)
    b = jnp.zeros((1, LANES), F32).at[0, :N_EXPERTS].set(b_expert).at[0, N_EXPERTS:N_EXPERTS + N_GROUPS].set(b_group)
    hi = w.astype(BF16)
    lo = (w - hi.astype(F32)).astype(BF16)
    return hi, lo, b


def _pad_cols(w, n):
    return jnp.pad(w, ((0, 0), (0, n - w.shape[1])))


def _pad_rows(w, n):
    return jnp.pad(w, ((0, n - w.shape[0]), (0, 0)))


def _tile(n, want):
    t = min(n, want)
    while n % t:
        t //= 2
    return t


def kernel(x, norm1_g, norm2_g, final_g, rwkv_mu, rwkv_w_rkv, rwkv_w0, rwkv_w1, rwkv_w2, rwkv_a0, rwkv_a1, rwkv_a2, rwkv_g1, rwkv_g2, rwkv_k_k, rwkv_k_a, rwkv_r_k, rwkv_ln_g, rwkv_ln_b, rwkv_w_o, conv_w_pw1, conv_b_pw1, conv_w_dw, conv_b_dw, conv_ln_g, conv_ln_b, conv_w_pw2, conv_b_pw2, moe_w_group, moe_b_group, moe_w_expert, moe_b_expert, moe_w_gate, moe_w_up, moe_w_down):
    B, T, D = x.shape
    N = B * T
    depth = norm1_g.shape[0]
    assert depth == 2 and D % QUAD == 0 and T % CHUNK == 0
    row = lambda v: v.reshape(1, -1)
    bf = lambda w: w.astype(BF16)

    r, k, v, wl, a, g = _rwkv_pre(
        x, row(norm1_g[0]), rwkv_mu[0], bf(rwkv_w_rkv[0]), row(rwkv_w0[0]),
        bf(_pad_cols(rwkv_w1[0], LANES)), bf(_pad_rows(rwkv_w2[0], LANES)), row(rwkv_a0[0]),
        bf(_pad_cols(rwkv_a1[0], LANES)), bf(_pad_rows(rwkv_a2[0], LANES)),
        bf(rwkv_g1[0]), bf(rwkv_g2[0]), tt=_tile(T, 256))
    y = _recurrence(r, k, v, wl, a, row(rwkv_k_k[0]), row(rwkv_k_a[0]), rwkv_r_k[0].reshape(1, -1),
                    row(rwkv_ln_g[0]), row(rwkv_ln_b[0]), tb=_tile(T, 256), width=D)
    wr_hi, wr_lo, br = _router_weights(moe_w_group[0], moe_b_group[0], moe_w_expert[0], moe_b_expert[0])
    h1, hn, gates = _rwkv_post(y.reshape(N, D), g.reshape(N, D), x.reshape(N, D), bf(rwkv_w_o[0]),
                               row(norm2_g[0]), wr_hi, wr_lo, br, tt=_tile(N, 512))
    h = _moe(hn, gates, h1, bf(moe_w_gate[0]), bf(moe_w_up[0]), bf(moe_w_down[0]), row(final_g),
             tm=_tile(N, 1024), eb=4, final_norm=False)

    wr_hi, wr_lo, br = _router_weights(moe_w_group[1], moe_b_group[1], moe_w_expert[1], moe_b_expert[1])
    h1, hn, gates = _conformer(
        h.reshape(B, T, D), row(norm1_g[1]), bf(conv_w_pw1[0]), row(conv_b_pw1[0]), conv_w_dw[0], row(conv_b_dw[0]),
        row(conv_ln_g[0]), row(conv_ln_b[0]), bf(conv_w_pw2[0]), row(conv_b_pw2[0]),
        row(norm2_g[1]), wr_hi, wr_lo, br, tt=_tile(T, 256))
    out = _moe(hn.reshape(N, D), gates.reshape(N, LANES), h1.reshape(N, D),
               bf(moe_w_gate[1]), bf(moe_w_up[1]), bf(moe_w_down[1]), row(final_g),
               tm=_tile(N, 1024), eb=4, final_norm=True)
    return out.reshape(B, T, D)
```

```python
import functools
import math

import jax
import jax.numpy as jnp
from jax import lax
from jax.experimental import pallas as pl
from jax.experimental.pallas import tpu as pltpu

F32 = jnp.float32
BF16 = jnp.bfloat16

HEAD = 64
LNX_EPS = HEAD * 1e-5
CONV_WIDTH = 31
CONV_LN_EPS = 1e-5
N_GROUPS = 4
EXPERTS_PER_GROUP = 8
N_EXPERTS = N_GROUPS * EXPERTS_PER_GROUP
RMS_EPS = 1e-6

LANES = 128
SUBLANES = 8
QUAD = 4 * HEAD
CHUNK = 64
VMEM_LIMIT = 56 * 1024 * 1024

NEG_BIG = -1e30


def _dot(a, b):
    return jnp.dot(a, b, preferred_element_type=F32)


def _dot_nt(a, b):
    return lax.dot_general(a, b, (((1,), (1,)), ((), ())), preferred_element_type=F32)


def _dot_tn(a, b):
    return lax.dot_general(a, b, (((0,), (0,)), ((), ())), preferred_element_type=F32)


def _split2(x):
    hi = x.astype(BF16)
    lo = (x - hi.astype(F32)).astype(BF16)
    return hi, lo


def _split3(x):
    hi = x.astype(BF16)
    r1 = x - hi.astype(F32)
    mid = r1.astype(BF16)
    lo = (r1 - mid.astype(F32)).astype(BF16)
    return hi, mid, lo


def _sigmoid(x):
    return 1.0 / (1.0 + jnp.exp(-x))


def _rms(x, g):
    return x * lax.rsqrt(jnp.mean(x * x, axis=-1, keepdims=True) + RMS_EPS) * g


def _rwkv_pre_kernel(x_ref, n1g_ref, mu_ref, wrkv_ref, w0_ref, w1_ref, w2_ref, a0_ref, a1_ref, a2_ref,
                     g1_ref, g2_ref, r_ref, k_ref, v_ref, wl_ref, a_ref, g_ref, prev_ref):
    i = pl.program_id(1)
    tt = x_ref.shape[0]
    hn = _rms(x_ref[...], n1g_ref[...])

    @pl.when(i == 0)
    def _():
        prev_ref[...] = jnp.zeros_like(prev_ref)

    prev = prev_ref[0:1, :]
    rolled = pltpu.roll(hn, 1, 0)
    row = lax.broadcasted_iota(jnp.int32, hn.shape, 0)
    shifted = jnp.where(row == 0, prev, rolled)
    prev_ref[0:1, :] = hn[tt - 1:tt, :]
    xx = shifted - hn

    def mix(j):
        return (hn + xx * mu_ref[j:j + 1, :]).astype(BF16)

    r_ref[...] = _dot(mix(0), wrkv_ref[0])
    k_ref[...] = _dot(mix(2), wrkv_ref[1])
    v_ref[...] = _dot(mix(3), wrkv_ref[2])
    zw = w0_ref[...] + _dot(jnp.tanh(_dot(mix(1), w1_ref[...])).astype(BF16), w2_ref[...])
    wl_ref[...] = -math.exp(-0.5) * _sigmoid(zw)
    a_ref[...] = _sigmoid(a0_ref[...] + _dot(_dot(mix(4), a1_ref[...]).astype(BF16), a2_ref[...]))
    g_ref[...] = _dot(_sigmoid(_dot(mix(5), g1_ref[...])).astype(BF16), g2_ref[...])


def _rwkv_pre(x, n1g, mu, wrkv, w0, w1, w2, a0, a1, a2, g1, g2, *, tt):
    B, T, D = x.shape
    tok = pl.BlockSpec((None, tt, D), lambda b, i: (b, i, 0))

    def full(arr):
        nd = arr.ndim
        return pl.BlockSpec(arr.shape, lambda b, i: (0,) * nd)

    params = (n1g, mu, wrkv, w0, w1, w2, a0, a1, a2, g1, g2)
    out = jax.ShapeDtypeStruct((B, T, D), F32)
    return pl.pallas_call(
        _rwkv_pre_kernel,
        out_shape=(out,) * 6,
        grid=(B, T // tt),
        in_specs=[tok] + [full(p) for p in params],
        out_specs=(tok,) * 6,
        scratch_shapes=[pltpu.VMEM((8, D), F32)],
        compiler_params=pltpu.CompilerParams(
            dimension_semantics=("arbitrary", "arbitrary"), vmem_limit_bytes=VMEM_LIMIT),
        name="rwkv_pre",
    )(x, *params)


def _rec_consts(tb):
    L, Q = CHUNK, QUAD
    ri = lax.broadcasted_iota(jnp.int32, (Q, Q), 0)
    ci = lax.broadcasted_iota(jnp.int32, (Q, Q), 1)
    bd = (jnp.right_shift(ri, 6) == jnp.right_shift(ci, 6)).astype(F32)
    ei = lax.broadcasted_iota(jnp.int32, (L, 4 * L), 0)
    ej = lax.broadcasted_iota(jnp.int32, (L, 4 * L), 1)
    eye_lp = (jnp.bitwise_and(ej, L - 1) == ei).astype(F32)
    mi = lax.broadcasted_iota(jnp.int32, (2 * L, 8 * L), 0)
    mj = lax.broadcasted_iota(jnp.int32, (2 * L, 8 * L), 1)
    t_row = jnp.bitwise_and(mi, L - 1)
    t_col = jnp.bitwise_and(mj, L - 1)
    score_mask = ((t_col < t_row) | ((mi >= L) & (t_col == t_row))).astype(F32)
    bi = lax.broadcasted_iota(jnp.int32, (tb, tb), 0)
    bj = lax.broadcasted_iota(jnp.int32, (tb, tb), 1)
    tri_bd = ((jnp.right_shift(bi, 6) == jnp.right_shift(bj, 6)) & (bj <= bi)).astype(BF16)
    return bd, eye_lp, score_mask, tri_bd


def _recurrence_kernel(r_ref, k_ref, v_ref, wl_ref, a_ref, kk_ref, ka_ref, rk_ref, lg_ref, lb_ref,
                       y_ref, z_ref, rt_s, at_s, kt_s, bt_s, kh_s, bh_s, vb_s, w_s, arb_s, u0_s, y0_s,
                       kbt_s, gcol_s):
    assert CHUNK == HEAD and QUAD == 4 * CHUNK
    L, Q = CHUNK, QUAD
    t = pl.program_id(2)
    tb, width = r_ref.shape
    nq, nc = width // Q, tb // L
    bd_f, eye_lp, score_mask_f, tri_bd = _rec_consts(tb)
    bd_b = bd_f.astype(BF16)

    def bdsum(x):
        hi, lo = _split2(x)
        return _dot(hi, bd_b) + _dot(lo, bd_b)

    def stack(xb):
        return jnp.concatenate([xb] * 4, axis=0) * bd_b

    @pl.when(t == 0)
    def _():
        z_ref[...] = jnp.zeros_like(z_ref)

    for q in range(nq):
        cols = slice(q * Q, (q + 1) * Q)
        r, k, v, wl, a = r_ref[:, cols], k_ref[:, cols], v_ref[:, cols], wl_ref[:, cols], a_ref[:, cols]
        kk = k * kk_ref[:, cols]
        kk = kk * lax.rsqrt(jnp.maximum(bdsum(kk * kk), 1e-24))
        k2 = k * (1.0 + (a - 1.0) * ka_ref[:, cols])
        bb = kk * a
        y_ref[:, cols] = bdsum(r * k2 * rk_ref[:, cols]) * v
        w_hi, w_mid, w_lo = _split3(wl)
        cw = _dot(tri_bd, w_hi) + _dot(tri_bd, w_mid) + _dot(tri_bd, w_lo)
        rt_s[:, cols] = (r * jnp.exp(cw)).astype(BF16)
        at_s[:, cols] = (-kk * jnp.exp(cw - wl)).astype(BF16)
        g_inv = jnp.exp(-cw)
        kt_s[:, cols] = (k2 * g_inv).astype(BF16)
        bt_s[:, cols] = (bb * g_inv).astype(BF16)
        vb_s[:, cols] = v.astype(BF16)
        for c in range(nc):
            rows = slice(c * L, (c + 1) * L)
            tot = cw[(c + 1) * L - 1:(c + 1) * L, :]
            g_rem = jnp.exp(tot - cw[rows, :])
            kh_s[rows, cols] = (k2[rows, :] * g_rem).astype(BF16)
            bh_s[rows, cols] = (bb[rows, :] * g_rem).astype(BF16)
            gcol_s[c * nq + q] = jnp.broadcast_to(jnp.transpose(jnp.exp(tot)), (Q, LANES))

    tiles = [(slice(c * L, (c + 1) * L), slice(q * Q, (q + 1) * Q), c * nq + q)
             for c in range(nc) for q in range(nq)]
    score_mask = score_mask_f > 0.5
    for rows, cols, j in tiles:
        rhs = jnp.concatenate([stack(kt_s[rows, cols]), stack(bt_s[rows, cols])], axis=0)
        lhs = jnp.concatenate([at_s[rows, cols], rt_s[rows, cols]], axis=0)
        sc = jnp.where(score_mask, _dot_nt(lhs, rhs), 0.0)
        kt_s[rows, cols] = sc[:L, :4 * L].astype(BF16)
        bt_s[rows, cols] = sc[L:, :4 * L].astype(BF16)
        arb_s[rows, cols] = sc[L:, 4 * L:].astype(BF16)
        a_ab = sc[:L, 4 * L:]
        w_s[rows, cols] = a_ab.astype(BF16)
        u0_s[rows, cols] = eye_lp + a_ab
        khbh = jnp.concatenate([kh_s[rows, cols], bh_s[rows, cols]], axis=0).astype(F32)
        kbt_s[j] = jnp.transpose(khbh).astype(BF16)
    for rows, cols, j in tiles:
        nb = w_s[rows, cols]
        w_s[rows, cols] = _dot(nb, stack(nb)).astype(BF16)
    for _ in range(int(math.log2(L)) - 2):
        for rows, cols, j in tiles:
            mb = w_s[rows, cols]
            tinv = u0_s[rows, cols]
            res = _dot(jnp.concatenate([mb, tinv.astype(BF16)], axis=0), stack(mb))
            w_s[rows, cols] = res[:L].astype(BF16)
            u0_s[rows, cols] = tinv + res[L:]
    for rows, cols, j in tiles:
        lhs = jnp.concatenate([kt_s[rows, cols], bt_s[rows, cols]], axis=0)
        pu_y0 = _dot(lhs, stack(vb_s[rows, cols]))
        kt_s[rows, cols] = pu_y0[:L].astype(BF16)
        y0_s[rows, cols] = pu_y0[L:]
    for rows, cols, j in tiles:
        tinv = u0_s[rows, cols]
        tinv = tinv + _dot(tinv.astype(BF16), stack(w_s[rows, cols]))
        wu = _dot(tinv.astype(BF16),
                  jnp.concatenate([stack(at_s[rows, cols]), stack(kt_s[rows, cols])], axis=1))
        w_s[rows, cols] = wu[:, :Q].astype(BF16)
        u0_s[rows, cols] = wu[:, Q:]

    def phase2(c, carry):
        rows = pl.ds(pl.multiple_of(c * L, L), L)
        for q in range(nq):
            cols = slice(q * Q, (q + 1) * Q)
            z = z_ref[q]
            wr = _dot(jnp.concatenate([w_s[rows, cols], rt_s[rows, cols]], axis=0), z.astype(BF16))
            usb = (wr[:L] + u0_s[rows, cols]).astype(BF16)
            y0_s[rows, cols] = y0_s[rows, cols] + wr[L:] + _dot(arb_s[rows, cols], stack(usb))
            upd = _dot(kbt_s[c * nq + q], jnp.concatenate([vb_s[rows, cols], usb], axis=0))
            g = gcol_s[c * nq + q]
            z_ref[q] = z * jnp.concatenate([g, g], axis=1) + upd * bd_f
        return carry

    lax.fori_loop(0, nc, phase2, 0)

    for q in range(nq):
        cols = slice(q * Q, (q + 1) * Q)
        y = y0_s[:, cols]
        yc = y - bdsum(y) * (1.0 / HEAD)
        var = bdsum(yc * yc) * (1.0 / HEAD)
        y_ref[:, cols] = y_ref[:, cols] + yc * lax.rsqrt(var + LNX_EPS) * lg_ref[:, cols] + lb_ref[:, cols]


def _recurrence(r, k, v, wl, a, k_k, k_a, r_k, ln_g, ln_b, *, tb, width):
    B, T, D = r.shape
    tok = pl.BlockSpec((None, tb, width), lambda b, g, t: (b, t, g))
    par = pl.BlockSpec((1, width), lambda b, g, t: (0, g))
    n_qc = (width // QUAD) * (tb // CHUNK)
    blk_bf = pltpu.VMEM((tb, width), BF16)
    blk_f32 = pltpu.VMEM((tb, width), F32)
    return pl.pallas_call(
        _recurrence_kernel,
        out_shape=jax.ShapeDtypeStruct((B, T, D), F32),
        grid=(B, D // width, T // tb),
        in_specs=[tok] * 5 + [par] * 5,
        out_specs=tok,
        scratch_shapes=[pltpu.VMEM((width // QUAD, QUAD, QUAD), F32)] + [blk_bf] * 9 + [blk_f32] * 2
        + [pltpu.VMEM((n_qc, QUAD, 2 * CHUNK), BF16), pltpu.VMEM((n_qc, QUAD, LANES), F32)],
        compiler_params=pltpu.CompilerParams(
            dimension_semantics=("arbitrary", "arbitrary", "arbitrary"), vmem_limit_bytes=VMEM_LIMIT),
        name="rwkv_recurrence",
    )(r, k, v, wl, a, k_k, k_a, r_k, ln_g, ln_b)


def _route(h1, n2g, wr_hi, wr_lo, br, xg_ref, route_ref, counts_ref, cnt_ref, is_first):
    d = h1.shape[1]
    hn = _rms(h1, n2g)
    x_hi, x_lo = _split2(hn)
    logits = _dot(x_hi, wr_hi) + _dot(x_lo, wr_hi) + _dot(x_hi, wr_lo) + br
    lane = lax.broadcasted_iota(jnp.int32, logits.shape, 1)

    def first_argmax(vals):
        mx = jnp.max(vals, axis=-1, keepdims=True)
        idx = jnp.min(jnp.where(vals == mx, lane, 4 * LANES), axis=-1, keepdims=True)
        return mx, idx

    lg = jnp.where((lane >= N_EXPERTS) & (lane < N_EXPERTS + N_GROUPS), logits, NEG_BIG)
    g_max, g_lane = first_argmax(lg)
    p_top = 1.0 / jnp.sum(jnp.exp(lg - g_max), axis=-1, keepdims=True)
    g_idx = g_lane - N_EXPERTS
    in_group = (lane >= g_idx * EXPERTS_PER_GROUP) & (lane < (g_idx + 1) * EXPERTS_PER_GROUP)
    le = jnp.where(in_group, logits, NEG_BIG)
    m1, i1 = first_argmax(le)
    m2, i2 = first_argmax(jnp.where(lane == i1, NEG_BIG, le))
    e2 = jnp.exp(m2 - m1)
    w1 = p_top / (1.0 + e2)
    w2 = p_top * e2 / (1.0 + e2)
    gates = jnp.where(lane == i1, w1, jnp.where(lane == i2, w2, 0.0))
    xg_ref[:, :d] = hn
    xg_ref[:, d:] = gates

    @pl.when(is_first)
    def _():
        cnt_ref[...] = jnp.zeros_like(cnt_ref)

    rows = h1.shape[0]
    onehot = lane == g_idx
    ri = lax.broadcasted_iota(jnp.int32, (rows, rows), 0)
    ci = lax.broadcasted_iota(jnp.int32, (rows, rows), 1)
    before = _dot((ci < ri).astype(F32).astype(BF16), onehot.astype(F32).astype(BF16)) + cnt_ref[0:1, :]
    rank = jnp.sum(jnp.where(onehot, before, 0.0), axis=-1, keepdims=True)
    route_ref[...] = jnp.where(lane == 0, g_idx.astype(F32), jnp.where(lane == 1, rank, 0.0))
    total = cnt_ref[0:1, :] + jnp.sum(onehot.astype(F32), axis=0, keepdims=True)
    cnt_ref[0:1, :] = total
    counts_ref[...] = total


def _rwkv_post_kernel(y_ref, g_ref, x_ref, wo_ref, n2g_ref, wrh_ref, wrl_ref, br_ref,
                      h1_ref, xg_ref, route_ref, counts_ref, cnt_ref):
    yg = (y_ref[...] * g_ref[...]).astype(BF16)
    h1 = x_ref[...] + _dot(yg, wo_ref[...])
    h1_ref[...] = h1
    _route(h1, n2g_ref[...], wrh_ref[...], wrl_ref[...], br_ref[...], xg_ref, route_ref, counts_ref, cnt_ref,
           pl.program_id(0) == 0)


def _tail_outputs(N, D, tt, tok_map):
    shapes = (jax.ShapeDtypeStruct((N, D), F32), jax.ShapeDtypeStruct((N, D + LANES), F32),
              jax.ShapeDtypeStruct((N, LANES), F32), jax.ShapeDtypeStruct((1, LANES), F32))
    specs = (pl.BlockSpec((tt, D), tok_map), pl.BlockSpec((tt, D + LANES), tok_map),
             pl.BlockSpec((tt, LANES), tok_map), pl.BlockSpec((1, LANES), lambda *_: (0, 0)))
    return shapes, specs


def _rwkv_post(y, g, x, wo, n2g, wr_hi, wr_lo, br, *, tt):
    N, D = x.shape
    tok_map = lambda i: (i, 0)
    tok = pl.BlockSpec((tt, D), tok_map)

    def full(arr):
        nd = arr.ndim
        return pl.BlockSpec(arr.shape, lambda i: (0,) * nd)

    params = (wo, n2g, wr_hi, wr_lo, br)
    out_shape, out_specs = _tail_outputs(N, D, tt, tok_map)
    return pl.pallas_call(
        _rwkv_post_kernel,
        out_shape=out_shape,
        grid=(N // tt,),
        in_specs=[tok, tok, tok] + [full(p) for p in params],
        out_specs=out_specs,
        scratch_shapes=[pltpu.VMEM((8, LANES), F32)],
        compiler_params=pltpu.CompilerParams(
            dimension_semantics=("arbitrary",), vmem_limit_bytes=VMEM_LIMIT),
        name="rwkv_post",
    )(y, g, x, *params)


HALO = 32


def _conformer_kernel(x_ref, n1g_ref, wpw1_ref, bpw1_ref, wdw_ref, bdw_ref, lng_ref, lnb_ref, wpw2_ref, bpw2_ref,
                      n2g_ref, wrh_ref, wrl_ref, br_ref, h1_ref, xg_ref, route_ref, counts_ref, ubuf_ref, cnt_ref):
    i = pl.program_id(1)
    tt, D = x_ref.shape
    x = x_ref[...]
    hn = _rms(x, n1g_ref[...]).astype(BF16)
    u = _dot(hn, wpw1_ref[...]) + bpw1_ref[...]
    u = u[:, :D] * _sigmoid(u[:, D:])

    @pl.when(i == 0)
    def _():
        ubuf_ref[0:HALO, :] = jnp.zeros((HALO, D), F32)
        ubuf_ref[HALO + tt:, :] = jnp.zeros((SUBLANES, D), F32)

    ubuf_ref[HALO:HALO + tt, :] = u
    base = HALO - (CONV_WIDTH - 1)
    acc = jnp.zeros((tt, D), F32) + bdw_ref[...]
    for s in range(SUBLANES):
        part = None
        for q in range((base + CONV_WIDTH - 1) // SUBLANES + 1):
            j = q * SUBLANES + s - base
            if 0 <= j < CONV_WIDTH:
                term = wdw_ref[j:j + 1, :] * ubuf_ref[q * SUBLANES:q * SUBLANES + tt + SUBLANES, :]
                part = term if part is None else part + term
        acc = acc + part[s:s + tt, :]
    ubuf_ref[0:HALO, :] = ubuf_ref[tt:tt + HALO, :]

    m = jnp.mean(acc, axis=-1, keepdims=True)
    c = acc - m
    var = jnp.mean(c * c, axis=-1, keepdims=True)
    c = c * lax.rsqrt(var + CONV_LN_EPS) * lng_ref[...] + lnb_ref[...]
    c = c * _sigmoid(c)
    h1 = x + _dot(c.astype(BF16), wpw2_ref[...]) + bpw2_ref[...]
    h1_ref[...] = h1
    _route(h1, n2g_ref[...], wrh_ref[...], wrl_ref[...], br_ref[...], xg_ref, route_ref, counts_ref, cnt_ref,
           (pl.program_id(0) == 0) & (i == 0))


def _conformer(x, n1g, wpw1, bpw1, wdw, bdw, lng, lnb, wpw2, bpw2, n2g, wr_hi, wr_lo, br, *, tt):
    B, T, D = x.shape
    nt = T // tt
    tok = pl.BlockSpec((None, tt, D), lambda b, i: (b, i, 0))

    def full(arr):
        nd = arr.ndim
        return pl.BlockSpec(arr.shape, lambda b, i: (0,) * nd)

    params = (n1g, wpw1, bpw1, wdw, bdw, lng, lnb, wpw2, bpw2, n2g, wr_hi, wr_lo, br)
    out_shape, out_specs = _tail_outputs(B * T, D, tt, lambda b, i: (b * nt + i, 0))
    return pl.pallas_call(
        _conformer_kernel,
        out_shape=out_shape,
        grid=(B, nt),
        in_specs=[tok] + [full(p) for p in params],
        out_specs=out_specs,
        scratch_shapes=[pltpu.VMEM((HALO + tt + SUBLANES, D), F32), pltpu.VMEM((8, LANES), F32)],
        compiler_params=pltpu.CompilerParams(
            dimension_semantics=("arbitrary", "arbitrary"), vmem_limit_bytes=VMEM_LIMIT),
        name="conformer",
    )(x, *params)


def _dispatch_kernel(pos_ref, xg_ref, init_ref, xs_ref, sem):
    del init_ref
    tt = xg_ref.shape[0]
    base = pl.program_id(0) * tt

    def issue(jj, carry):
        j0 = pl.multiple_of(jj * SUBLANES, SUBLANES)
        for u in range(SUBLANES):
            src = xg_ref.at[pl.ds(j0, SUBLANES)].at[pl.ds(u, 1)]
            pltpu.make_async_copy(src, xs_ref.at[pl.ds(pos_ref[base + j0 + u], 1)], sem).start()
        return carry

    lax.fori_loop(0, tt // SUBLANES, issue, 0)
    pltpu.make_async_copy(xg_ref, xs_ref.at[pl.ds(0, tt)], sem).wait()


def _dispatch(pos, xg, n_rows, *, tt):
    N, W = xg.shape
    return pl.pallas_call(
        _dispatch_kernel,
        out_shape=jax.ShapeDtypeStruct((n_rows, W), F32),
        grid_spec=pltpu.PrefetchScalarGridSpec(
            num_scalar_prefetch=1,
            grid=(N // tt,),
            in_specs=[pl.BlockSpec((tt, W), lambda i, pos: (i, 0)), pl.BlockSpec(memory_space=pl.ANY)],
            out_specs=pl.BlockSpec(memory_space=pl.ANY),
            scratch_shapes=[pltpu.SemaphoreType.DMA]),
        input_output_aliases={2: 0},
        compiler_params=pltpu.CompilerParams(
            dimension_semantics=("arbitrary",), vmem_limit_bytes=VMEM_LIMIT, disable_bounds_checks=True),
        name="moe_dispatch",
    )(pos, xg, jnp.zeros((n_rows, W), F32))


def _moe_group_kernel(tg_ref, nu_ref, xs_ref, wg_ref, wu_ref, wd_ref, ys_ref, *, eb):
    i = pl.program_id(0)
    e = pl.program_id(1)
    d = ys_ref.shape[1]

    @pl.when(i < nu_ref[0])
    def _():
        x = xs_ref[:, :d].astype(BF16)
        gates = xs_ref[:, d:]
        lane = lax.broadcasted_iota(jnp.int32, gates.shape, 1)
        first = tg_ref[i] * EXPERTS_PER_GROUP + e * eb
        zs = []
        for q in range(eb):
            ge = jnp.sum(jnp.where(lane == first + q, gates, 0.0), axis=-1, keepdims=True)
            zg = _dot(x, wg_ref[q])
            zu = _dot(x, wu_ref[q])
            zs.append((zg * _sigmoid(zg) * zu * ge).astype(BF16))
        contrib = _dot(jnp.concatenate(zs, axis=-1), wd_ref[...].reshape(-1, d))

        @pl.when(e == 0)
        def _():
            ys_ref[...] = contrib

        @pl.when(e != 0)
        def _():
            ys_ref[...] += contrib

    @pl.when((i >= nu_ref[0]) & (e == 0))
    def _():
        ys_ref[...] = jnp.zeros_like(ys_ref)


def _moe_group(tile_group, n_used, xs, wg, wu, wd, *, tm, eb):
    n_rows, W = xs.shape
    E, D, F = wg.shape
    steps = EXPERTS_PER_GROUP // eb
    row_map = lambda i, e, tg, nu: (jnp.minimum(i, nu[0] - 1), 0)
    w_map = lambda i, e, tg, nu: (tg[i] * steps + e, 0, 0)
    return pl.pallas_call(
        functools.partial(_moe_group_kernel, eb=eb),
        out_shape=jax.ShapeDtypeStruct((n_rows, D), F32),
        grid_spec=pltpu.PrefetchScalarGridSpec(
            num_scalar_prefetch=2,
            grid=(n_rows // tm, steps),
            in_specs=[pl.BlockSpec((tm, W), row_map),
                      pl.BlockSpec((eb, D, F), w_map), pl.BlockSpec((eb, D, F), w_map),
                      pl.BlockSpec((eb, F, D), w_map)],
            out_specs=pl.BlockSpec((tm, D), lambda i, e, tg, nu: (i, 0))),
        compiler_params=pltpu.CompilerParams(
            dimension_semantics=("arbitrary", "arbitrary"), vmem_limit_bytes=VMEM_LIMIT),
        name="moe_group",
    )(tile_group, n_used, xs, wg, wu, wd)


def _combine_kernel(pos_ref, h1_ref, fg_ref, ys_ref, out_ref, buf_ref, sem, *, final_norm):
    tt = h1_ref.shape[0]
    base = pl.program_id(0) * tt

    def issue(jj, carry):
        j0 = pl.multiple_of(jj * SUBLANES, SUBLANES)
        for u in range(SUBLANES):
            dst = buf_ref.at[pl.ds(j0, SUBLANES)].at[pl.ds(u, 1)]
            pltpu.make_async_copy(ys_ref.at[pl.ds(pos_ref[base + j0 + u], 1)], dst, sem).start()
        return carry

    lax.fori_loop(0, tt // SUBLANES, issue, 0)
    pltpu.make_async_copy(ys_ref.at[pl.ds(0, tt)], buf_ref, sem).wait()
    h = h1_ref[...] + buf_ref[...]
    out_ref[...] = _rms(h, fg_ref[...]) if final_norm else h


def _combine(pos, h1, fg, ys, *, tt, final_norm):
    N, D = h1.shape
    tok = pl.BlockSpec((tt, D), lambda i, pos: (i, 0))
    return pl.pallas_call(
        functools.partial(_combine_kernel, final_norm=final_norm),
        out_shape=jax.ShapeDtypeStruct((N, D), F32),
        grid_spec=pltpu.PrefetchScalarGridSpec(
            num_scalar_prefetch=1,
            grid=(N // tt,),
            in_specs=[tok, pl.BlockSpec((1, D), lambda i, pos: (0, 0)), pl.BlockSpec(memory_space=pl.ANY)],
            out_specs=tok,
            scratch_shapes=[pltpu.VMEM((tt, D), F32), pltpu.SemaphoreType.DMA]),
        compiler_params=pltpu.CompilerParams(
            dimension_semantics=("arbitrary",), vmem_limit_bytes=VMEM_LIMIT, disable_bounds_checks=True),
        name="moe_combine_final" if final_norm else "moe_combine",
    )(pos, h1, fg, ys)


def _moe(route, counts, xg, h1, wg, wu, wd, fg, *, tm, tt, eb, final_norm):
    N, D = h1.shape
    group = route[:, 0].astype(jnp.int32)
    rank = route[:, 1].astype(jnp.int32)
    cnt = counts[0, :N_GROUPS].astype(jnp.int32)
    padded = (cnt + tm - 1) // tm * tm
    ends = jnp.cumsum(padded)
    pos = (ends - padded)[group] + rank
    n_tiles = N // tm + N_GROUPS
    n_used = ends[-1:] // tm
    tile_group = jnp.sum((jnp.arange(n_tiles)[:, None] * tm >= ends[None, :]).astype(jnp.int32), axis=1)
    tile_group = jnp.minimum(tile_group, tile_group[n_used[0] - 1])
    xs = _dispatch(pos, xg, n_tiles * tm, tt=tt)
    ys = _moe_group(tile_group, n_used, xs, wg, wu, wd, tm=tm, eb=eb)
    return _combine(pos, h1, fg, ys, tt=tt, final_norm=final_norm)


def _router_weights(w_group, b_group, w_expert, b_expert):
    D = w_group.shape[0]
    w = jnp.zeros((D, LANES), F32).at[:, :N_EXPERTS].set(w_expert).at[:, N_EXPERTS:N_EXPERTS + N_GROUPS].set(w_group)
    b = jnp.zeros((1, LANES), F32).at[0, :N_EXPERTS].set(b_expert).at[0, N_EXPERTS:N_EXPERTS + N_GROUPS].set(b_group)
    hi = w.astype(BF16)
    lo = (w - hi.astype(F32)).astype(BF16)
    return hi, lo, b


def _pad_cols(w, n):
    return jnp.pad(w, ((0, 0), (0, n - w.shape[1])))


def _pad_rows(w, n):
    return jnp.pad(w, ((0, n - w.shape[0]), (0, 0)))


def _tile(n, want):
    t = min(n, want)
    while n % t:
        t //= 2
    return t


def kernel(x, norm1_g, norm2_g, final_g, rwkv_mu, rwkv_w_rkv, rwkv_w0, rwkv_w1, rwkv_w2, rwkv_a0, rwkv_a1, rwkv_a2, rwkv_g1, rwkv_g2, rwkv_k_k, rwkv_k_a, rwkv_r_k, rwkv_ln_g, rwkv_ln_b, rwkv_w_o, conv_w_pw1, conv_b_pw1, conv_w_dw, conv_b_dw, conv_ln_g, conv_ln_b, conv_w_pw2, conv_b_pw2, moe_w_group, moe_b_group, moe_w_expert, moe_b_expert, moe_w_gate, moe_w_up, moe_w_down):
    B, T, D = x.shape
    N = B * T
    depth = norm1_g.shape[0]
    assert depth == 2 and D % QUAD == 0 and T % CHUNK == 0
    row = lambda v: v.reshape(1, -1)
    bf = lambda w: w.astype(BF16)

    r, k, v, wl, a, g = _rwkv_pre(
        x, row(norm1_g[0]), rwkv_mu[0], bf(rwkv_w_rkv[0]), row(rwkv_w0[0]),
        bf(_pad_cols(rwkv_w1[0], LANES)), bf(_pad_rows(rwkv_w2[0], LANES)), row(rwkv_a0[0]),
        bf(_pad_cols(rwkv_a1[0], LANES)), bf(_pad_rows(rwkv_a2[0], LANES)),
        bf(rwkv_g1[0]), bf(rwkv_g2[0]), tt=_tile(T, 256))
    y = _recurrence(r, k, v, wl, a, row(rwkv_k_k[0]), row(rwkv_k_a[0]), rwkv_r_k[0].reshape(1, -1),
                    row(rwkv_ln_g[0]), row(rwkv_ln_b[0]), tb=_tile(T, 256), width=D)
    wr_hi, wr_lo, br = _router_weights(moe_w_group[0], moe_b_group[0], moe_w_expert[0], moe_b_expert[0])
    h1, xg, route, counts = _rwkv_post(y.reshape(N, D), g.reshape(N, D), x.reshape(N, D), bf(rwkv_w_o[0]),
                                       row(norm2_g[0]), wr_hi, wr_lo, br, tt=_tile(N, 512))
    moe_tiles = dict(tm=_tile(N, 1024), tt=_tile(N, 512), eb=4)
    h = _moe(route, counts, xg, h1, bf(moe_w_gate[0]), bf(moe_w_up[0]), bf(moe_w_down[0]), row(final_g),
             final_norm=False, **moe_tiles)

    wr_hi, wr_lo, br = _router_weights(moe_w_group[1], moe_b_group[1], moe_w_expert[1], moe_b_expert[1])
    h1, xg, route, counts = _conformer(
        h.reshape(B, T, D), row(norm1_g[1]), bf(conv_w_pw1[0]), row(conv_b_pw1[0]), conv_w_dw[0], row(conv_b_dw[0]),
        row(conv_ln_g[0]), row(conv_ln_b[0]), bf(conv_w_pw2[0]), row(conv_b_pw2[0]),
        row(norm2_g[1]), wr_hi, wr_lo, br, tt=_tile(T, 256))
    out = _moe(route, counts, xg, h1, bf(moe_w_gate[1]), bf(moe_w_up[1]), bf(moe_w_down[1]), row(final_g),
               final_norm=True, **moe_tiles)
    return out.reshape(B, T, D)
```

```python
import functools
import math

import jax
import jax.numpy as jnp
from jax import lax
from jax.experimental import pallas as pl
from jax.experimental.pallas import tpu as pltpu

F32 = jnp.float32
BF16 = jnp.bfloat16

HEAD = 64
LNX_EPS = HEAD * 1e-5
CONV_WIDTH = 31
CONV_LN_EPS = 1e-5
N_GROUPS = 4
EXPERTS_PER_GROUP = 8
N_EXPERTS = N_GROUPS * EXPERTS_PER_GROUP
RMS_EPS = 1e-6

LANES = 128
SUBLANES = 8
QUAD = 4 * HEAD
CHUNK = 64
VMEM_LIMIT = 56 * 1024 * 1024

NEG_BIG = -1e30


def _dot(a, b):
    return jnp.dot(a, b, preferred_element_type=F32)


def _dot_nt(a, b):
    return lax.dot_general(a, b, (((1,), (1,)), ((), ())), preferred_element_type=F32)


def _dot_tn(a, b):
    return lax.dot_general(a, b, (((0,), (0,)), ((), ())), preferred_element_type=F32)


def _split2(x):
    hi = x.astype(BF16)
    lo = (x - hi.astype(F32)).astype(BF16)
    return hi, lo


def _split3(x):
    hi = x.astype(BF16)
    r1 = x - hi.astype(F32)
    mid = r1.astype(BF16)
    lo = (r1 - mid.astype(F32)).astype(BF16)
    return hi, mid, lo


def _sigmoid(x):
    return 1.0 / (1.0 + jnp.exp(-x))


def _rms(x, g):
    return x * lax.rsqrt(jnp.mean(x * x, axis=-1, keepdims=True) + RMS_EPS) * g


def _rwkv_pre_kernel(x_ref, n1g_ref, mu_ref, wrkv_ref, w0_ref, w1_ref, w2_ref, a0_ref, a1_ref, a2_ref,
                     g1_ref, g2_ref, r_ref, k_ref, v_ref, wl_ref, a_ref, g_ref, prev_ref):
    i = pl.program_id(1)
    tt = x_ref.shape[0]
    hn = _rms(x_ref[...], n1g_ref[...])

    @pl.when(i == 0)
    def _():
        prev_ref[...] = jnp.zeros_like(prev_ref)

    prev = prev_ref[0:1, :]
    rolled = pltpu.roll(hn, 1, 0)
    row = lax.broadcasted_iota(jnp.int32, hn.shape, 0)
    shifted = jnp.where(row == 0, prev, rolled)
    prev_ref[0:1, :] = hn[tt - 1:tt, :]
    xx = shifted - hn

    def mix(j):
        return (hn + xx * mu_ref[j:j + 1, :]).astype(BF16)

    r_ref[...] = _dot(mix(0), wrkv_ref[0]).astype(BF16)
    k_ref[...] = _dot(mix(2), wrkv_ref[1]).astype(BF16)
    v_ref[...] = _dot(mix(3), wrkv_ref[2]).astype(BF16)
    zw = w0_ref[...] + _dot(jnp.tanh(_dot(mix(1), w1_ref[...])).astype(BF16), w2_ref[...])
    wl_ref[...] = -math.exp(-0.5) * _sigmoid(zw)
    a = _sigmoid(a0_ref[...] + _dot(_dot(mix(4), a1_ref[...]).astype(BF16), a2_ref[...]))
    a_ref[...] = a.astype(BF16)
    g_ref[...] = _dot(_sigmoid(_dot(mix(5), g1_ref[...])).astype(BF16), g2_ref[...]).astype(BF16)


def _rwkv_pre(x, n1g, mu, wrkv, w0, w1, w2, a0, a1, a2, g1, g2, *, tt):
    B, T, D = x.shape
    tok = pl.BlockSpec((None, tt, D), lambda b, i: (b, i, 0))

    def full(arr):
        nd = arr.ndim
        return pl.BlockSpec(arr.shape, lambda b, i: (0,) * nd)

    params = (n1g, mu, wrkv, w0, w1, w2, a0, a1, a2, g1, g2)
    out = lambda dt: jax.ShapeDtypeStruct((B, T, D), dt)
    return pl.pallas_call(
        _rwkv_pre_kernel,
        out_shape=(out(BF16), out(BF16), out(BF16), out(F32), out(BF16), out(BF16)),
        grid=(B, T // tt),
        in_specs=[tok] + [full(p) for p in params],
        out_specs=(tok,) * 6,
        scratch_shapes=[pltpu.VMEM((8, D), F32)],
        compiler_params=pltpu.CompilerParams(
            dimension_semantics=("arbitrary", "arbitrary"), vmem_limit_bytes=VMEM_LIMIT),
        name="rwkv_pre",
    )(x, *params)


def _rec_consts(tb):
    L, Q = CHUNK, QUAD
    ri = lax.broadcasted_iota(jnp.int32, (Q, Q), 0)
    ci = lax.broadcasted_iota(jnp.int32, (Q, Q), 1)
    bd = (jnp.right_shift(ri, 6) == jnp.right_shift(ci, 6)).astype(F32)
    ei = lax.broadcasted_iota(jnp.int32, (L, 4 * L), 0)
    ej = lax.broadcasted_iota(jnp.int32, (L, 4 * L), 1)
    eye_lp = (jnp.bitwise_and(ej, L - 1) == ei).astype(F32)
    mi = lax.broadcasted_iota(jnp.int32, (2 * L, 8 * L), 0)
    mj = lax.broadcasted_iota(jnp.int32, (2 * L, 8 * L), 1)
    t_row = jnp.bitwise_and(mi, L - 1)
    t_col = jnp.bitwise_and(mj, L - 1)
    score_mask = ((t_col < t_row) | ((mi >= L) & (t_col == t_row))).astype(F32)
    bi = lax.broadcasted_iota(jnp.int32, (tb, tb), 0)
    bj = lax.broadcasted_iota(jnp.int32, (tb, tb), 1)
    tri_bd = ((jnp.right_shift(bi, 6) == jnp.right_shift(bj, 6)) & (bj <= bi)).astype(BF16)
    return bd, eye_lp, score_mask, tri_bd


def _recurrence_kernel(r_ref, k_ref, v_ref, wl_ref, a_ref, kk_ref, ka_ref, rk_ref, lg_ref, lb_ref,
                       y_ref, z_ref, rt_s, at_s, kt_s, bt_s, kh_s, bh_s, vb_s, w_s, arb_s, u0_s, y0_s, bon_s,
                       kbt_s, gcol_s):
    assert CHUNK == HEAD and QUAD == 4 * CHUNK
    L, Q = CHUNK, QUAD
    t = pl.program_id(0)
    nc, _, width = r_ref.shape
    tb = nc * L
    nq = width // Q

    def load(ref, cols):
        return jnp.concatenate([ref[b, :, cols].astype(F32) for b in range(nc)], axis=0)
    bd_f, eye_lp, score_mask_f, tri_bd = _rec_consts(tb)
    bd_b = bd_f.astype(BF16)

    def bdsum(x):
        hi, lo = _split2(x)
        return _dot(hi, bd_b) + _dot(lo, bd_b)

    def stack(xb):
        return jnp.concatenate([xb] * 4, axis=0) * bd_b

    @pl.when(t == 0)
    def _():
        z_ref[...] = jnp.zeros_like(z_ref)

    for q in range(nq):
        cols = slice(q * Q, (q + 1) * Q)
        r, k, v, wl, a = (load(ref, cols) for ref in (r_ref, k_ref, v_ref, wl_ref, a_ref))
        kk = k * kk_ref[:, cols]
        kk = kk * lax.rsqrt(jnp.maximum(bdsum(kk * kk), 1e-24))
        k2 = k * (1.0 + (a - 1.0) * ka_ref[:, cols])
        bb = kk * a
        bon_s[:, cols] = bdsum(r * k2 * rk_ref[:, cols]) * v
        w_hi, w_mid, w_lo = _split3(wl)
        cw = _dot(tri_bd, w_hi) + _dot(tri_bd, w_mid) + _dot(tri_bd, w_lo)
        rt_s[:, cols] = (r * jnp.exp(cw)).astype(BF16)
        at_s[:, cols] = (-kk * jnp.exp(cw - wl)).astype(BF16)
        g_inv = jnp.exp(-cw)
        kt_s[:, cols] = (k2 * g_inv).astype(BF16)
        bt_s[:, cols] = (bb * g_inv).astype(BF16)
        vb_s[:, cols] = v.astype(BF16)
        for c in range(nc):
            rows = slice(c * L, (c + 1) * L)
            tot = cw[(c + 1) * L - 1:(c + 1) * L, :]
            g_rem = jnp.exp(tot - cw[rows, :])
            kh_s[rows, cols] = (k2[rows, :] * g_rem).astype(BF16)
            bh_s[rows, cols] = (bb[rows, :] * g_rem).astype(BF16)
            gcol_s[c * nq + q] = jnp.broadcast_to(jnp.transpose(jnp.exp(tot)), (Q, LANES))

    tiles = [(slice(c * L, (c + 1) * L), slice(q * Q, (q + 1) * Q), c * nq + q)
             for c in range(nc) for q in range(nq)]
    score_mask = score_mask_f > 0.5
    for rows, cols, j in tiles:
        rhs = jnp.concatenate([stack(kt_s[rows, cols]), stack(bt_s[rows, cols])], axis=0)
        lhs = jnp.concatenate([at_s[rows, cols], rt_s[rows, cols]], axis=0)
        sc = jnp.where(score_mask, _dot_nt(lhs, rhs), 0.0)
        kt_s[rows, cols] = sc[:L, :4 * L].astype(BF16)
        bt_s[rows, cols] = sc[L:, :4 * L].astype(BF16)
        arb_s[rows, cols] = sc[L:, 4 * L:].astype(BF16)
        a_ab = sc[:L, 4 * L:]
        w_s[rows, cols] = a_ab.astype(BF16)
        u0_s[rows, cols] = eye_lp + a_ab
        khbh = jnp.concatenate([kh_s[rows, cols], bh_s[rows, cols]], axis=0).astype(F32)
        kbt_s[j] = jnp.transpose(khbh).astype(BF16)
    for rows, cols, j in tiles:
        nb = w_s[rows, cols]
        w_s[rows, cols] = _dot(nb, stack(nb)).astype(BF16)
    for _ in range(int(math.log2(L)) - 2):
        for rows, cols, j in tiles:
            mb = w_s[rows, cols]
            tinv = u0_s[rows, cols]
            res = _dot(jnp.concatenate([mb, tinv.astype(BF16)], axis=0), stack(mb))
            w_s[rows, cols] = res[:L].astype(BF16)
            u0_s[rows, cols] = tinv + res[L:]
    for rows, cols, j in tiles:
        lhs = jnp.concatenate([kt_s[rows, cols], bt_s[rows, cols]], axis=0)
        pu_y0 = _dot(lhs, stack(vb_s[rows, cols]))
        kt_s[rows, cols] = pu_y0[:L].astype(BF16)
        y0_s[rows, cols] = pu_y0[L:]
    for rows, cols, j in tiles:
        tinv = u0_s[rows, cols]
        tinv = tinv + _dot(tinv.astype(BF16), stack(w_s[rows, cols]))
        wu = _dot(tinv.astype(BF16),
                  jnp.concatenate([stack(at_s[rows, cols]), stack(kt_s[rows, cols])], axis=1))
        w_s[rows, cols] = wu[:, :Q].astype(BF16)
        u0_s[rows, cols] = wu[:, Q:]

    for rows, cols, j in tiles:
        z = z_ref[j]
        wr = _dot(jnp.concatenate([w_s[rows, cols], rt_s[rows, cols]], axis=0), z.astype(BF16))
        usb = (wr[:L] + u0_s[rows, cols]).astype(BF16)
        y0_s[rows, cols] = y0_s[rows, cols] + wr[L:] + _dot(arb_s[rows, cols], stack(usb))
        upd = _dot(kbt_s[j], jnp.concatenate([vb_s[rows, cols], usb], axis=0))
        g = gcol_s[j]
        z_ref[j] = z * jnp.concatenate([g, g], axis=1) + upd * bd_f

    for q in range(nq):
        cols = slice(q * Q, (q + 1) * Q)
        y = y0_s[:, cols]
        yc = y - bdsum(y) * (1.0 / HEAD)
        var = bdsum(yc * yc) * (1.0 / HEAD)
        out = bon_s[:, cols] + yc * lax.rsqrt(var + LNX_EPS) * lg_ref[:, cols] + lb_ref[:, cols]
        for b in range(nc):
            y_ref[b, :, cols] = out[b * L:(b + 1) * L, :].astype(y_ref.dtype)


def _recurrence(r, k, v, wl, a, k_k, k_a, r_k, ln_g, ln_b):
    B, T, D = r.shape
    tok = pl.BlockSpec((B, CHUNK, D), lambda t: (0, t, 0))
    par = pl.BlockSpec((1, D), lambda t: (0, 0))
    n_tiles = B * (D // QUAD)
    blk_bf = pltpu.VMEM((B * CHUNK, D), BF16)
    blk_f32 = pltpu.VMEM((B * CHUNK, D), F32)
    return pl.pallas_call(
        _recurrence_kernel,
        out_shape=jax.ShapeDtypeStruct((B, T, D), BF16),
        grid=(T // CHUNK,),
        in_specs=[tok] * 5 + [par] * 5,
        out_specs=tok,
        scratch_shapes=[pltpu.VMEM((n_tiles, QUAD, QUAD), F32)] + [blk_bf] * 9 + [blk_f32] * 3
        + [pltpu.VMEM((n_tiles, QUAD, 2 * CHUNK), BF16), pltpu.VMEM((n_tiles, QUAD, LANES), F32)],
        compiler_params=pltpu.CompilerParams(
            dimension_semantics=("arbitrary",), vmem_limit_bytes=VMEM_LIMIT),
        name="rwkv_recurrence",
    )(r, k, v, wl, a, k_k, k_a, r_k, ln_g, ln_b)


def _route(h1, n2g, wr_hi, wr_lo, br, xg_ref, route_ref, counts_ref, cnt_ref, is_first):
    d = h1.shape[1]
    hn = _rms(h1, n2g)
    x_hi, x_lo = _split2(hn)
    logits = _dot(x_hi, wr_hi) + _dot(x_lo, wr_hi) + _dot(x_hi, wr_lo) + br
    lane = lax.broadcasted_iota(jnp.int32, logits.shape, 1)

    def first_argmax(vals):
        mx = jnp.max(vals, axis=-1, keepdims=True)
        idx = jnp.min(jnp.where(vals == mx, lane, 4 * LANES), axis=-1, keepdims=True)
        return mx, idx

    lg = jnp.where((lane >= N_EXPERTS) & (lane < N_EXPERTS + N_GROUPS), logits, NEG_BIG)
    g_max, g_lane = first_argmax(lg)
    p_top = 1.0 / jnp.sum(jnp.exp(lg - g_max), axis=-1, keepdims=True)
    g_idx = g_lane - N_EXPERTS
    in_group = (lane >= g_idx * EXPERTS_PER_GROUP) & (lane < (g_idx + 1) * EXPERTS_PER_GROUP)
    le = jnp.where(in_group, logits, NEG_BIG)
    m1, i1 = first_argmax(le)
    m2, i2 = first_argmax(jnp.where(lane == i1, NEG_BIG, le))
    e2 = jnp.exp(m2 - m1)
    w1 = p_top / (1.0 + e2)
    w2 = p_top * e2 / (1.0 + e2)
    gates = jnp.where(lane == i1, w1, jnp.where(lane == i2, w2, 0.0))
    xg_ref[:, :d] = hn
    xg_ref[:, d:] = gates

    @pl.when(is_first)
    def _():
        cnt_ref[...] = jnp.zeros_like(cnt_ref)

    rows = h1.shape[0]
    onehot = lane == g_idx
    ri = lax.broadcasted_iota(jnp.int32, (rows, rows), 0)
    ci = lax.broadcasted_iota(jnp.int32, (rows, rows), 1)
    before = _dot((ci < ri).astype(F32).astype(BF16), onehot.astype(F32).astype(BF16)) + cnt_ref[0:1, :]
    rank = jnp.sum(jnp.where(onehot, before, 0.0), axis=-1, keepdims=True)
    route_ref[...] = jnp.where(lane == 0, g_idx.astype(F32), jnp.where(lane == 1, rank, 0.0))
    total = cnt_ref[0:1, :] + jnp.sum(onehot.astype(F32), axis=0, keepdims=True)
    cnt_ref[0:1, :] = total
    counts_ref[...] = total


def _rwkv_post_kernel(y_ref, g_ref, x_ref, wo_ref, n2g_ref, wrh_ref, wrl_ref, br_ref,
                      h1_ref, xg_ref, route_ref, counts_ref, cnt_ref):
    yg = (y_ref[...] * g_ref[...]).astype(BF16)
    h1 = x_ref[...] + _dot(yg, wo_ref[...])
    h1_ref[...] = h1
    _route(h1, n2g_ref[...], wrh_ref[...], wrl_ref[...], br_ref[...], xg_ref, route_ref, counts_ref, cnt_ref,
           pl.program_id(0) == 0)


def _tail_outputs(N, D, tt, tok_map):
    shapes = (jax.ShapeDtypeStruct((N, D), F32), jax.ShapeDtypeStruct((N, D + LANES), F32),
              jax.ShapeDtypeStruct((N, LANES), F32), jax.ShapeDtypeStruct((1, LANES), F32))
    specs = (pl.BlockSpec((tt, D), tok_map), pl.BlockSpec((tt, D + LANES), tok_map),
             pl.BlockSpec((tt, LANES), tok_map), pl.BlockSpec((1, LANES), lambda *_: (0, 0)))
    return shapes, specs


def _rwkv_post(y, g, x, wo, n2g, wr_hi, wr_lo, br, *, tt):
    N, D = x.shape
    tok_map = lambda i: (i, 0)
    tok = pl.BlockSpec((tt, D), tok_map)

    def full(arr):
        nd = arr.ndim
        return pl.BlockSpec(arr.shape, lambda i: (0,) * nd)

    params = (wo, n2g, wr_hi, wr_lo, br)
    out_shape, out_specs = _tail_outputs(N, D, tt, tok_map)
    return pl.pallas_call(
        _rwkv_post_kernel,
        out_shape=out_shape,
        grid=(N // tt,),
        in_specs=[tok, tok, tok] + [full(p) for p in params],
        out_specs=out_specs,
        scratch_shapes=[pltpu.VMEM((8, LANES), F32)],
        compiler_params=pltpu.CompilerParams(
            dimension_semantics=("arbitrary",), vmem_limit_bytes=VMEM_LIMIT),
        name="rwkv_post",
    )(y, g, x, *params)


HALO = 32


def _conformer_kernel(x_ref, n1g_ref, wpw1_ref, bpw1_ref, wdw_ref, bdw_ref, lng_ref, lnb_ref, wpw2_ref, bpw2_ref,
                      n2g_ref, wrh_ref, wrl_ref, br_ref, h1_ref, xg_ref, route_ref, counts_ref, ubuf_ref, cnt_ref):
    i = pl.program_id(1)
    tt, D = x_ref.shape
    x = x_ref[...]
    hn = _rms(x, n1g_ref[...]).astype(BF16)
    u = _dot(hn, wpw1_ref[...]) + bpw1_ref[...]
    u = u[:, :D] * _sigmoid(u[:, D:])

    @pl.when(i == 0)
    def _():
        ubuf_ref[0:HALO, :] = jnp.zeros((HALO, D), F32)
        ubuf_ref[HALO + tt:, :] = jnp.zeros((SUBLANES, D), F32)

    ubuf_ref[HALO:HALO + tt, :] = u
    base = HALO - (CONV_WIDTH - 1)
    acc = jnp.zeros((tt, D), F32) + bdw_ref[...]
    for s in range(SUBLANES):
        part = None
        for q in range((base + CONV_WIDTH - 1) // SUBLANES + 1):
            j = q * SUBLANES + s - base
            if 0 <= j < CONV_WIDTH:
                term = wdw_ref[j:j + 1, :] * ubuf_ref[q * SUBLANES:q * SUBLANES + tt + SUBLANES, :]
                part = term if part is None else part + term
        acc = acc + part[s:s + tt, :]
    ubuf_ref[0:HALO, :] = ubuf_ref[tt:tt + HALO, :]

    m = jnp.mean(acc, axis=-1, keepdims=True)
    c = acc - m
    var = jnp.mean(c * c, axis=-1, keepdims=True)
    c = c * lax.rsqrt(var + CONV_LN_EPS) * lng_ref[...] + lnb_ref[...]
    c = c * _sigmoid(c)
    h1 = x + _dot(c.astype(BF16), wpw2_ref[...]) + bpw2_ref[...]
    h1_ref[...] = h1
    _route(h1, n2g_ref[...], wrh_ref[...], wrl_ref[...], br_ref[...], xg_ref, route_ref, counts_ref, cnt_ref,
           (pl.program_id(0) == 0) & (i == 0))


def _conformer(x, n1g, wpw1, bpw1, wdw, bdw, lng, lnb, wpw2, bpw2, n2g, wr_hi, wr_lo, br, *, tt):
    B, T, D = x.shape
    nt = T // tt
    tok = pl.BlockSpec((None, tt, D), lambda b, i: (b, i, 0))

    def full(arr):
        nd = arr.ndim
        return pl.BlockSpec(arr.shape, lambda b, i: (0,) * nd)

    params = (n1g, wpw1, bpw1, wdw, bdw, lng, lnb, wpw2, bpw2, n2g, wr_hi, wr_lo, br)
    out_shape, out_specs = _tail_outputs(B * T, D, tt, lambda b, i: (b * nt + i, 0))
    return pl.pallas_call(
        _conformer_kernel,
        out_shape=out_shape,
        grid=(B, nt),
        in_specs=[tok] + [full(p) for p in params],
        out_specs=out_specs,
        scratch_shapes=[pltpu.VMEM((HALO + tt + SUBLANES, D), F32), pltpu.VMEM((8, LANES), F32)],
        compiler_params=pltpu.CompilerParams(
            dimension_semantics=("arbitrary", "arbitrary"), vmem_limit_bytes=VMEM_LIMIT),
        name="conformer",
    )(x, *params)


def _dispatch_kernel(pos_ref, xg_ref, init_ref, xs_ref, sem):
    del init_ref
    tt = xg_ref.shape[0]
    base = pl.program_id(0) * tt

    def issue(jj, carry):
        j0 = pl.multiple_of(jj * SUBLANES, SUBLANES)
        for u in range(SUBLANES):
            src = xg_ref.at[pl.ds(j0, SUBLANES)].at[pl.ds(u, 1)]
            pltpu.make_async_copy(src, xs_ref.at[pl.ds(pos_ref[base + j0 + u], 1)], sem).start()
        return carry

    lax.fori_loop(0, tt // SUBLANES, issue, 0)
    pltpu.make_async_copy(xg_ref, xs_ref.at[pl.ds(0, tt)], sem).wait()


def _dispatch(pos, xg, n_rows, *, tt):
    N, W = xg.shape
    return pl.pallas_call(
        _dispatch_kernel,
        out_shape=jax.ShapeDtypeStruct((n_rows, W), F32),
        grid_spec=pltpu.PrefetchScalarGridSpec(
            num_scalar_prefetch=1,
            grid=(N // tt,),
            in_specs=[pl.BlockSpec((tt, W), lambda i, pos: (i, 0)), pl.BlockSpec(memory_space=pl.ANY)],
            out_specs=pl.BlockSpec(memory_space=pl.ANY),
            scratch_shapes=[pltpu.SemaphoreType.DMA]),
        input_output_aliases={2: 0},
        compiler_params=pltpu.CompilerParams(
            dimension_semantics=("arbitrary",), vmem_limit_bytes=VMEM_LIMIT, disable_bounds_checks=True),
        name="moe_dispatch",
    )(pos, xg, jnp.zeros((n_rows, W), F32))


def _moe_group_kernel(tg_ref, nu_ref, xs_ref, wg_ref, wu_ref, wd_ref, ys_ref, *, eb):
    i = pl.program_id(0)
    e = pl.program_id(1)
    d = ys_ref.shape[1]

    @pl.when(i < nu_ref[0])
    def _():
        x = xs_ref[:, :d].astype(BF16)
        gates = xs_ref[:, d:]
        lane = lax.broadcasted_iota(jnp.int32, gates.shape, 1)
        first = tg_ref[i] * EXPERTS_PER_GROUP + e * eb
        zs = []
        for q in range(eb):
            ge = jnp.sum(jnp.where(lane == first + q, gates, 0.0), axis=-1, keepdims=True)
            zg = _dot(x, wg_ref[q])
            zu = _dot(x, wu_ref[q])
            zs.append((zg * _sigmoid(zg) * zu * ge).astype(BF16))
        contrib = _dot(jnp.concatenate(zs, axis=-1), wd_ref[...].reshape(-1, d))

        @pl.when(e == 0)
        def _():
            ys_ref[...] = contrib

        @pl.when(e != 0)
        def _():
            ys_ref[...] += contrib

    @pl.when((i >= nu_ref[0]) & (e == 0))
    def _():
        ys_ref[...] = jnp.zeros_like(ys_ref)


def _moe_group(tile_group, n_used, xs, wg, wu, wd, *, layer, tm, eb):
    n_rows, W = xs.shape
    _, E, D, F = wg.shape
    steps = EXPERTS_PER_GROUP // eb
    row_map = lambda i, e, tg, nu: (jnp.minimum(i, nu[0] - 1), 0)
    w_map = lambda i, e, tg, nu: (layer, tg[i] * steps + e, 0, 0)
    return pl.pallas_call(
        functools.partial(_moe_group_kernel, eb=eb),
        out_shape=jax.ShapeDtypeStruct((n_rows, D), F32),
        grid_spec=pltpu.PrefetchScalarGridSpec(
            num_scalar_prefetch=2,
            grid=(n_rows // tm, steps),
            in_specs=[pl.BlockSpec((tm, W), row_map),
                      pl.BlockSpec((None, eb, D, F), w_map), pl.BlockSpec((None, eb, D, F), w_map),
                      pl.BlockSpec((None, eb, F, D), w_map)],
            out_specs=pl.BlockSpec((tm, D), lambda i, e, tg, nu: (i, 0))),
        compiler_params=pltpu.CompilerParams(
            dimension_semantics=("arbitrary", "arbitrary"), vmem_limit_bytes=VMEM_LIMIT),
        name="moe_group",
    )(tile_group, n_used, xs, wg, wu, wd)


def _combine_kernel(pos_ref, h1_ref, fg_ref, ys_ref, out_ref, buf_ref, sem, *, final_norm):
    tt = h1_ref.shape[0]
    base = pl.program_id(0) * tt

    def issue(jj, carry):
        j0 = pl.multiple_of(jj * SUBLANES, SUBLANES)
        for u in range(SUBLANES):
            dst = buf_ref.at[pl.ds(j0, SUBLANES)].at[pl.ds(u, 1)]
            pltpu.make_async_copy(ys_ref.at[pl.ds(pos_ref[base + j0 + u], 1)], dst, sem).start()
        return carry

    lax.fori_loop(0, tt // SUBLANES, issue, 0)
    pltpu.make_async_copy(ys_ref.at[pl.ds(0, tt)], buf_ref, sem).wait()
    h = h1_ref[...] + buf_ref[...]
    out_ref[...] = _rms(h, fg_ref[...]) if final_norm else h


def _combine(pos, h1, fg, ys, *, tt, final_norm):
    N, D = h1.shape
    tok = pl.BlockSpec((tt, D), lambda i, pos: (i, 0))
    return pl.pallas_call(
        functools.partial(_combine_kernel, final_norm=final_norm),
        out_shape=jax.ShapeDtypeStruct((N, D), F32),
        grid_spec=pltpu.PrefetchScalarGridSpec(
            num_scalar_prefetch=1,
            grid=(N // tt,),
            in_specs=[tok, pl.BlockSpec((1, D), lambda i, pos: (0, 0)), pl.BlockSpec(memory_space=pl.ANY)],
            out_specs=tok,
            scratch_shapes=[pltpu.VMEM((tt, D), F32), pltpu.SemaphoreType.DMA]),
        compiler_params=pltpu.CompilerParams(
            dimension_semantics=("arbitrary",), vmem_limit_bytes=VMEM_LIMIT, disable_bounds_checks=True),
        name="moe_combine_final" if final_norm else "moe_combine",
    )(pos, h1, fg, ys)


def _moe(route, counts, xg, h1, wg, wu, wd, fg, *, layer, tm, tt, eb, final_norm):
    N, D = h1.shape
    group = route[:, 0].astype(jnp.int32)
    rank = route[:, 1].astype(jnp.int32)
    cnt = counts[0, :N_GROUPS].astype(jnp.int32)
    padded = (cnt + tm - 1) // tm * tm
    ends = jnp.cumsum(padded)
    pos = (ends - padded)[group] + rank
    n_tiles = N // tm + N_GROUPS
    n_used = ends[-1:] // tm
    tile_group = jnp.sum((jnp.arange(n_tiles)[:, None] * tm >= ends[None, :]).astype(jnp.int32), axis=1)
    tile_group = jnp.minimum(tile_group, tile_group[n_used[0] - 1])
    xs = _dispatch(pos, xg, n_tiles * tm, tt=tt)
    ys = _moe_group(tile_group, n_used, xs, wg, wu, wd, layer=layer, tm=tm, eb=eb)
    return _combine(pos, h1, fg, ys, tt=tt, final_norm=final_norm)


def _router_weights(w_group, b_group, w_expert, b_expert):
    D = w_group.shape[0]
    w = jnp.zeros((D, LANES), F32).at[:, :N_EXPERTS].set(w_expert).at[:, N_EXPERTS:N_EXPERTS + N_GROUPS].set(w_group)
    b = jnp.zeros((1, LANES), F32).at[0, :N_EXPERTS].set(b_expert).at[0, N_EXPERTS:N_EXPERTS + N_GROUPS].set(b_group)
    hi = w.astype(BF16)
    lo = (w - hi.astype(F32)).astype(BF16)
    return hi, lo, b


def _pad_cols(w, n):
    return jnp.pad(w, ((0, 0), (0, n - w.shape[1])))


def _pad_rows(w, n):
    return jnp.pad(w, ((0, n - w.shape[0]), (0, 0)))


def _tile(n, want):
    t = min(n, want)
    while n % t:
        t //= 2
    return t


def kernel(x, norm1_g, norm2_g, final_g, rwkv_mu, rwkv_w_rkv, rwkv_w0, rwkv_w1, rwkv_w2, rwkv_a0, rwkv_a1, rwkv_a2, rwkv_g1, rwkv_g2, rwkv_k_k, rwkv_k_a, rwkv_r_k, rwkv_ln_g, rwkv_ln_b, rwkv_w_o, conv_w_pw1, conv_b_pw1, conv_w_dw, conv_b_dw, conv_ln_g, conv_ln_b, conv_w_pw2, conv_b_pw2, moe_w_group, moe_b_group, moe_w_expert, moe_b_expert, moe_w_gate, moe_w_up, moe_w_down):
    B, T, D = x.shape
    N = B * T
    depth = norm1_g.shape[0]
    assert depth == 2 and D % QUAD == 0 and T % CHUNK == 0
    row = lambda v: v.reshape(1, -1)
    bf = lambda w: w.astype(BF16)

    r, k, v, wl, a, g = _rwkv_pre(
        x, row(norm1_g[0]), rwkv_mu[0], bf(rwkv_w_rkv[0]), row(rwkv_w0[0]),
        bf(_pad_cols(rwkv_w1[0], LANES)), bf(_pad_rows(rwkv_w2[0], LANES)), row(rwkv_a0[0]),
        bf(_pad_cols(rwkv_a1[0], LANES)), bf(_pad_rows(rwkv_a2[0], LANES)),
        bf(rwkv_g1[0]), bf(rwkv_g2[0]), tt=_tile(T, 512))
    y = _recurrence(r, k, v, wl, a, row(rwkv_k_k[0]), row(rwkv_k_a[0]), rwkv_r_k[0].reshape(1, -1),
                    row(rwkv_ln_g[0]), row(rwkv_ln_b[0]))
    wr_hi, wr_lo, br = _router_weights(moe_w_group[0], moe_b_group[0], moe_w_expert[0], moe_b_expert[0])
    h1, xg, route, counts = _rwkv_post(y.reshape(N, D), g.reshape(N, D), x.reshape(N, D), bf(rwkv_w_o[0]),
                                       row(norm2_g[0]), wr_hi, wr_lo, br, tt=_tile(N, 512))
    moe_tiles = dict(tm=_tile(N, 1024), tt=_tile(N, 512), eb=4)
    moe_w = (bf(moe_w_gate), bf(moe_w_up), bf(moe_w_down))
    h = _moe(route, counts, xg, h1, *moe_w, row(final_g),
             layer=0, final_norm=False, **moe_tiles)

    wr_hi, wr_lo, br = _router_weights(moe_w_group[1], moe_b_group[1], moe_w_expert[1], moe_b_expert[1])
    h1, xg, route, counts = _conformer(
        h.reshape(B, T, D), row(norm1_g[1]), bf(conv_w_pw1[0]), row(conv_b_pw1[0]), conv_w_dw[0], row(conv_b_dw[0]),
        row(conv_ln_g[0]), row(conv_ln_b[0]), bf(conv_w_pw2[0]), row(conv_b_pw2[0]),
        row(norm2_g[1]), wr_hi, wr_lo, br, tt=_tile(T, 256))
    out = _moe(route, counts, xg, h1, *moe_w, row(final_g),
               layer=1, final_norm=True, **moe_tiles)
    return out.reshape(B, T, D)
```

```python
import functools
import math

import jax
import jax.numpy as jnp
from jax import lax
from jax.experimental import pallas as pl
from jax.experimental.pallas import tpu as pltpu

F32 = jnp.float32
BF16 = jnp.bfloat16

HEAD = 64
LNX_EPS = HEAD * 1e-5
CONV_WIDTH = 31
CONV_LN_EPS = 1e-5
N_GROUPS = 4
EXPERTS_PER_GROUP = 8
N_EXPERTS = N_GROUPS * EXPERTS_PER_GROUP
RMS_EPS = 1e-6

LANES = 128
SUBLANES = 8
QUAD = 4 * HEAD
CHUNK = 64
VMEM_LIMIT = 56 * 1024 * 1024

NEG_BIG = -1e30


def _dot(a, b):
    return jnp.dot(a, b, preferred_element_type=F32)


def _dot_nt(a, b):
    return lax.dot_general(a, b, (((1,), (1,)), ((), ())), preferred_element_type=F32)


def _dot_tn(a, b):
    return lax.dot_general(a, b, (((0,), (0,)), ((), ())), preferred_element_type=F32)


def _split2(x):
    hi = x.astype(BF16)
    lo = (x - hi.astype(F32)).astype(BF16)
    return hi, lo


def _sigmoid(x):
    return 1.0 / (1.0 + jnp.exp(-x))


def _rms(x, g):
    return x * lax.rsqrt(jnp.mean(x * x, axis=-1, keepdims=True) + RMS_EPS) * g


def _rwkv_pre_kernel(x_ref, n1g_ref, mu_ref, wrkv_ref, w0_ref, w1_ref, w2_ref, a0_ref, a1_ref, a2_ref,
                     g1_ref, g2_ref, r_ref, k_ref, v_ref, wl_ref, a_ref, g_ref, prev_ref):
    i = pl.program_id(1)
    tt = x_ref.shape[0]
    hn = _rms(x_ref[...], n1g_ref[...])

    @pl.when(i == 0)
    def _():
        prev_ref[...] = jnp.zeros_like(prev_ref)

    prev = prev_ref[0:1, :]
    rolled = pltpu.roll(hn, 1, 0)
    row = lax.broadcasted_iota(jnp.int32, hn.shape, 0)
    shifted = jnp.where(row == 0, prev, rolled)
    prev_ref[0:1, :] = hn[tt - 1:tt, :]
    xx = shifted - hn

    def mix(j):
        return (hn + xx * mu_ref[j:j + 1, :]).astype(BF16)

    r_ref[...] = _dot(mix(0), wrkv_ref[0]).astype(BF16)
    k_ref[...] = _dot(mix(2), wrkv_ref[1]).astype(BF16)
    v_ref[...] = _dot(mix(3), wrkv_ref[2]).astype(BF16)
    zw = w0_ref[...] + _dot(jnp.tanh(_dot(mix(1), w1_ref[...])).astype(BF16), w2_ref[...])
    wl_ref[...] = -math.exp(-0.5) * _sigmoid(zw)
    a = _sigmoid(a0_ref[...] + _dot(_dot(mix(4), a1_ref[...]).astype(BF16), a2_ref[...]))
    a_ref[...] = a.astype(BF16)
    g_ref[...] = _dot(_sigmoid(_dot(mix(5), g1_ref[...])).astype(BF16), g2_ref[...]).astype(BF16)


def _rwkv_pre(x, n1g, mu, wrkv, w0, w1, w2, a0, a1, a2, g1, g2, *, tt):
    B, T, D = x.shape
    tok = pl.BlockSpec((None, tt, D), lambda b, i: (b, i, 0))

    def full(arr):
        nd = arr.ndim
        return pl.BlockSpec(arr.shape, lambda b, i: (0,) * nd)

    params = (n1g, mu, wrkv, w0, w1, w2, a0, a1, a2, g1, g2)
    out = lambda dt: jax.ShapeDtypeStruct((B, T, D), dt)
    return pl.pallas_call(
        _rwkv_pre_kernel,
        out_shape=(out(BF16), out(BF16), out(BF16), out(F32), out(BF16), out(BF16)),
        grid=(B, T // tt),
        in_specs=[tok] + [full(p) for p in params],
        out_specs=(tok,) * 6,
        scratch_shapes=[pltpu.VMEM((8, D), F32)],
        compiler_params=pltpu.CompilerParams(
            dimension_semantics=("arbitrary", "arbitrary"), vmem_limit_bytes=VMEM_LIMIT),
        name="rwkv_pre",
    )(x, *params)


def _rec_consts(tb):
    L, Q = CHUNK, QUAD
    ri = lax.broadcasted_iota(jnp.int32, (Q, Q), 0)
    ci = lax.broadcasted_iota(jnp.int32, (Q, Q), 1)
    bd = (jnp.right_shift(ri, 6) == jnp.right_shift(ci, 6)).astype(F32)
    ei = lax.broadcasted_iota(jnp.int32, (L, 4 * L), 0)
    ej = lax.broadcasted_iota(jnp.int32, (L, 4 * L), 1)
    eye_lp = (jnp.bitwise_and(ej, L - 1) == ei).astype(F32)
    mi = lax.broadcasted_iota(jnp.int32, (2 * L, 8 * L), 0)
    mj = lax.broadcasted_iota(jnp.int32, (2 * L, 8 * L), 1)
    t_row = jnp.bitwise_and(mi, L - 1)
    t_col = jnp.bitwise_and(mj, L - 1)
    score_mask = ((t_col < t_row) | ((mi >= L) & (t_col == t_row))).astype(F32)
    bi = lax.broadcasted_iota(jnp.int32, (tb, tb), 0)
    bj = lax.broadcasted_iota(jnp.int32, (tb, tb), 1)
    tri_bd = ((jnp.right_shift(bi, 6) == jnp.right_shift(bj, 6)) & (bj <= bi)).astype(BF16)
    return bd, eye_lp, score_mask, tri_bd


def _recurrence_kernel(r_ref, k_ref, v_ref, wl_ref, a_ref, kk_ref, ka_ref, rk_ref, lg_ref, lb_ref,
                       y_ref, z_ref, rt_s, at_s, kt_s, bt_s, kh_s, bh_s, vb_s, w_s, arb_s, u0_s, y0_s, bon_s,
                       kbt_s, gcol_s):
    assert CHUNK == HEAD and QUAD == 4 * CHUNK
    L, Q = CHUNK, QUAD
    t = pl.program_id(0)
    nc, _, width = r_ref.shape
    tb = nc * L
    nq = width // Q

    def load(ref, cols):
        return jnp.concatenate([ref[b, :, cols].astype(F32) for b in range(nc)], axis=0)
    bd_f, eye_lp, score_mask_f, tri_bd = _rec_consts(tb)
    bd_b = bd_f.astype(BF16)

    def bdsum(x, passes=1):
        if passes == 1:
            return _dot(x.astype(BF16), bd_b)
        hi, lo = _split2(x)
        return _dot(hi, bd_b) + _dot(lo, bd_b)

    def stack(xb):
        return jnp.concatenate([xb] * 4, axis=0) * bd_b

    @pl.when(t == 0)
    def _():
        z_ref[...] = jnp.zeros_like(z_ref)

    for q in range(nq):
        cols = slice(q * Q, (q + 1) * Q)
        r, k, v, wl, a = (load(ref, cols) for ref in (r_ref, k_ref, v_ref, wl_ref, a_ref))
        kk = k * kk_ref[:, cols]
        kk = kk * lax.rsqrt(jnp.maximum(bdsum(kk * kk, passes=2), 1e-24))
        k2 = k * (1.0 + (a - 1.0) * ka_ref[:, cols])
        bb = kk * a
        bon_s[:, cols] = bdsum(r * k2 * rk_ref[:, cols]) * v
        w_hi, w_lo = _split2(wl)
        cw = _dot(tri_bd, w_hi) + _dot(tri_bd, w_lo)
        rt_s[:, cols] = (r * jnp.exp(cw)).astype(BF16)
        at_s[:, cols] = (-kk * jnp.exp(cw - wl)).astype(BF16)
        g_inv = jnp.exp(-cw)
        kt_s[:, cols] = (k2 * g_inv).astype(BF16)
        bt_s[:, cols] = (bb * g_inv).astype(BF16)
        vb_s[:, cols] = v.astype(BF16)
        for c in range(nc):
            rows = slice(c * L, (c + 1) * L)
            tot = cw[(c + 1) * L - 1:(c + 1) * L, :]
            g_rem = jnp.exp(tot - cw[rows, :])
            kh_s[rows, cols] = (k2[rows, :] * g_rem).astype(BF16)
            bh_s[rows, cols] = (bb[rows, :] * g_rem).astype(BF16)
            gcol_s[c * nq + q] = jnp.broadcast_to(jnp.transpose(jnp.exp(tot)), (Q, LANES))

    tiles = [(slice(c * L, (c + 1) * L), slice(q * Q, (q + 1) * Q), c * nq + q)
             for c in range(nc) for q in range(nq)]
    score_mask = score_mask_f > 0.5
    for rows, cols, j in tiles:
        rhs = jnp.concatenate([stack(kt_s[rows, cols]), stack(bt_s[rows, cols])], axis=0)
        lhs = jnp.concatenate([at_s[rows, cols], rt_s[rows, cols]], axis=0)
        sc = jnp.where(score_mask, _dot_nt(lhs, rhs), 0.0)
        kt_s[rows, cols] = sc[:L, :4 * L].astype(BF16)
        bt_s[rows, cols] = sc[L:, :4 * L].astype(BF16)
        arb_s[rows, cols] = sc[L:, 4 * L:].astype(BF16)
        a_ab = sc[:L, 4 * L:]
        w_s[rows, cols] = a_ab.astype(BF16)
        u0_s[rows, cols] = eye_lp + a_ab
        khbh = jnp.concatenate([kh_s[rows, cols], bh_s[rows, cols]], axis=0).astype(F32)
        kbt_s[j] = jnp.transpose(khbh).astype(BF16)
    for rows, cols, j in tiles:
        nb = w_s[rows, cols]
        w_s[rows, cols] = _dot(nb, stack(nb)).astype(BF16)
    for _ in range(int(math.log2(L)) - 2):
        for rows, cols, j in tiles:
            mb = w_s[rows, cols]
            tinv = u0_s[rows, cols]
            res = _dot(jnp.concatenate([mb, tinv.astype(BF16)], axis=0), stack(mb))
            w_s[rows, cols] = res[:L].astype(BF16)
            u0_s[rows, cols] = tinv + res[L:]
    for rows, cols, j in tiles:
        lhs = jnp.concatenate([kt_s[rows, cols], bt_s[rows, cols]], axis=0)
        pu_y0 = _dot(lhs, stack(vb_s[rows, cols]))
        kt_s[rows, cols] = pu_y0[:L].astype(BF16)
        y0_s[rows, cols] = pu_y0[L:]
    for rows, cols, j in tiles:
        tinv = u0_s[rows, cols]
        tinv = tinv + _dot(tinv.astype(BF16), stack(w_s[rows, cols]))
        wu = _dot(tinv.astype(BF16),
                  jnp.concatenate([stack(at_s[rows, cols]), stack(kt_s[rows, cols])], axis=1))
        w_s[rows, cols] = wu[:, :Q].astype(BF16)
        u0_s[rows, cols] = wu[:, Q:]

    for rows, cols, j in tiles:
        z = z_ref[j]
        wr = _dot(jnp.concatenate([w_s[rows, cols], rt_s[rows, cols]], axis=0), z.astype(BF16))
        usb = (wr[:L] + u0_s[rows, cols]).astype(BF16)
        y0_s[rows, cols] = y0_s[rows, cols] + wr[L:] + _dot(arb_s[rows, cols], stack(usb))
        upd = _dot(kbt_s[j], jnp.concatenate([vb_s[rows, cols], usb], axis=0))
        g = gcol_s[j]
        z_ref[j] = z * jnp.concatenate([g, g], axis=1) + upd * bd_f

    for q in range(nq):
        cols = slice(q * Q, (q + 1) * Q)
        y = y0_s[:, cols]
        yc = y - bdsum(y) * (1.0 / HEAD)
        var = bdsum(yc * yc) * (1.0 / HEAD)
        out = bon_s[:, cols] + yc * lax.rsqrt(var + LNX_EPS) * lg_ref[:, cols] + lb_ref[:, cols]
        for b in range(nc):
            y_ref[b, :, cols] = out[b * L:(b + 1) * L, :].astype(y_ref.dtype)


def _recurrence(r, k, v, wl, a, k_k, k_a, r_k, ln_g, ln_b):
    B, T, D = r.shape
    tok = pl.BlockSpec((B, CHUNK, D), lambda t: (0, t, 0))
    par = pl.BlockSpec((1, D), lambda t: (0, 0))
    n_tiles = B * (D // QUAD)
    blk_bf = pltpu.VMEM((B * CHUNK, D), BF16)
    blk_f32 = pltpu.VMEM((B * CHUNK, D), F32)
    return pl.pallas_call(
        _recurrence_kernel,
        out_shape=jax.ShapeDtypeStruct((B, T, D), BF16),
        grid=(T // CHUNK,),
        in_specs=[tok] * 5 + [par] * 5,
        out_specs=tok,
        scratch_shapes=[pltpu.VMEM((n_tiles, QUAD, QUAD), F32)] + [blk_bf] * 9 + [blk_f32] * 3
        + [pltpu.VMEM((n_tiles, QUAD, 2 * CHUNK), BF16), pltpu.VMEM((n_tiles, QUAD, LANES), F32)],
        compiler_params=pltpu.CompilerParams(
            dimension_semantics=("arbitrary",), vmem_limit_bytes=VMEM_LIMIT),
        name="rwkv_recurrence",
    )(r, k, v, wl, a, k_k, k_a, r_k, ln_g, ln_b)


def _route(h1, n2g, wr_hi, wr_lo, br, xg_ref, route_ref, counts_ref, cnt_ref, is_first):
    d = h1.shape[1]
    hn = _rms(h1, n2g)
    x_hi, x_lo = _split2(hn)
    logits = _dot(x_hi, wr_hi) + _dot(x_lo, wr_hi) + _dot(x_hi, wr_lo) + br
    lane = lax.broadcasted_iota(jnp.int32, logits.shape, 1)

    def first_argmax(vals):
        mx = jnp.max(vals, axis=-1, keepdims=True)
        idx = jnp.min(jnp.where(vals == mx, lane, 4 * LANES), axis=-1, keepdims=True)
        return mx, idx

    lg = jnp.where((lane >= N_EXPERTS) & (lane < N_EXPERTS + N_GROUPS), logits, NEG_BIG)
    g_max, g_lane = first_argmax(lg)
    p_top = 1.0 / jnp.sum(jnp.exp(lg - g_max), axis=-1, keepdims=True)
    g_idx = g_lane - N_EXPERTS
    in_group = (lane >= g_idx * EXPERTS_PER_GROUP) & (lane < (g_idx + 1) * EXPERTS_PER_GROUP)
    le = jnp.where(in_group, logits, NEG_BIG)
    m1, i1 = first_argmax(le)
    m2, i2 = first_argmax(jnp.where(lane == i1, NEG_BIG, le))
    e2 = jnp.exp(m2 - m1)
    w1 = p_top / (1.0 + e2)
    w2 = p_top * e2 / (1.0 + e2)
    gates = jnp.where(lane == i1, w1, jnp.where(lane == i2, w2, 0.0))
    xg_ref[:, :d] = hn
    xg_ref[:, d:] = gates

    @pl.when(is_first)
    def _():
        cnt_ref[...] = jnp.zeros_like(cnt_ref)

    rows = h1.shape[0]
    onehot = lane == g_idx
    ri = lax.broadcasted_iota(jnp.int32, (rows, rows), 0)
    ci = lax.broadcasted_iota(jnp.int32, (rows, rows), 1)
    before = _dot((ci < ri).astype(F32).astype(BF16), onehot.astype(F32).astype(BF16)) + cnt_ref[0:1, :]
    rank = jnp.sum(jnp.where(onehot, before, 0.0), axis=-1, keepdims=True)
    route_ref[...] = jnp.where(lane == 0, g_idx.astype(F32), jnp.where(lane == 1, rank, 0.0))
    total = cnt_ref[0:1, :] + jnp.sum(onehot.astype(F32), axis=0, keepdims=True)
    cnt_ref[0:1, :] = total
    counts_ref[...] = total


def _rwkv_post_kernel(y_ref, g_ref, x_ref, wo_ref, n2g_ref, wrh_ref, wrl_ref, br_ref,
                      h1_ref, xg_ref, route_ref, counts_ref, cnt_ref):
    yg = (y_ref[...] * g_ref[...]).astype(BF16)
    h1 = x_ref[...] + _dot(yg, wo_ref[...])
    h1_ref[...] = h1
    _route(h1, n2g_ref[...], wrh_ref[...], wrl_ref[...], br_ref[...], xg_ref, route_ref, counts_ref, cnt_ref,
           pl.program_id(0) == 0)


def _tail_outputs(N, D, tt, tok_map):
    shapes = (jax.ShapeDtypeStruct((N, D), F32), jax.ShapeDtypeStruct((N, D + LANES), F32),
              jax.ShapeDtypeStruct((N, LANES), F32), jax.ShapeDtypeStruct((1, LANES), F32))
    specs = (pl.BlockSpec((tt, D), tok_map), pl.BlockSpec((tt, D + LANES), tok_map),
             pl.BlockSpec((tt, LANES), tok_map), pl.BlockSpec((1, LANES), lambda *_: (0, 0)))
    return shapes, specs


def _rwkv_post(y, g, x, wo, n2g, wr_hi, wr_lo, br, *, tt):
    N, D = x.shape
    tok_map = lambda i: (i, 0)
    tok = pl.BlockSpec((tt, D), tok_map)

    def full(arr):
        nd = arr.ndim
        return pl.BlockSpec(arr.shape, lambda i: (0,) * nd)

    params = (wo, n2g, wr_hi, wr_lo, br)
    out_shape, out_specs = _tail_outputs(N, D, tt, tok_map)
    return pl.pallas_call(
        _rwkv_post_kernel,
        out_shape=out_shape,
        grid=(N // tt,),
        in_specs=[tok, tok, tok] + [full(p) for p in params],
        out_specs=out_specs,
        scratch_shapes=[pltpu.VMEM((8, LANES), F32)],
        compiler_params=pltpu.CompilerParams(
            dimension_semantics=("arbitrary",), vmem_limit_bytes=VMEM_LIMIT),
        name="rwkv_post",
    )(y, g, x, *params)


HALO = 32


def _conformer_kernel(x_ref, n1g_ref, wpw1_ref, bpw1_ref, wdw_ref, bdw_ref, lng_ref, lnb_ref, wpw2_ref, bpw2_ref,
                      n2g_ref, wrh_ref, wrl_ref, br_ref, h1_ref, xg_ref, route_ref, counts_ref, ubuf_ref, cnt_ref):
    i = pl.program_id(1)
    tt, D = x_ref.shape
    x = x_ref[...]
    hn = _rms(x, n1g_ref[...]).astype(BF16)
    u = _dot(hn, wpw1_ref[...]) + bpw1_ref[...]
    u = u[:, :D] * _sigmoid(u[:, D:])

    @pl.when(i == 0)
    def _():
        ubuf_ref[0:HALO, :] = jnp.zeros((HALO, D), F32)
        ubuf_ref[HALO + tt:, :] = jnp.zeros((SUBLANES, D), F32)

    ubuf_ref[HALO:HALO + tt, :] = u
    base = HALO - (CONV_WIDTH - 1)
    acc = jnp.zeros((tt, D), F32) + bdw_ref[...]
    for s in range(SUBLANES):
        part = None
        for q in range((base + CONV_WIDTH - 1) // SUBLANES + 1):
            j = q * SUBLANES + s - base
            if 0 <= j < CONV_WIDTH:
                term = wdw_ref[j:j + 1, :] * ubuf_ref[q * SUBLANES:q * SUBLANES + tt + SUBLANES, :]
                part = term if part is None else part + term
        acc = acc + part[s:s + tt, :]
    ubuf_ref[0:HALO, :] = ubuf_ref[tt:tt + HALO, :]

    m = jnp.mean(acc, axis=-1, keepdims=True)
    c = acc - m
    var = jnp.mean(c * c, axis=-1, keepdims=True)
    c = c * lax.rsqrt(var + CONV_LN_EPS) * lng_ref[...] + lnb_ref[...]
    c = c * _sigmoid(c)
    h1 = x + _dot(c.astype(BF16), wpw2_ref[...]) + bpw2_ref[...]
    h1_ref[...] = h1
    _route(h1, n2g_ref[...], wrh_ref[...], wrl_ref[...], br_ref[...], xg_ref, route_ref, counts_ref, cnt_ref,
           (pl.program_id(0) == 0) & (i == 0))


def _conformer(x, n1g, wpw1, bpw1, wdw, bdw, lng, lnb, wpw2, bpw2, n2g, wr_hi, wr_lo, br, *, tt):
    B, T, D = x.shape
    nt = T // tt
    tok = pl.BlockSpec((None, tt, D), lambda b, i: (b, i, 0))

    def full(arr):
        nd = arr.ndim
        return pl.BlockSpec(arr.shape, lambda b, i: (0,) * nd)

    params = (n1g, wpw1, bpw1, wdw, bdw, lng, lnb, wpw2, bpw2, n2g, wr_hi, wr_lo, br)
    out_shape, out_specs = _tail_outputs(B * T, D, tt, lambda b, i: (b * nt + i, 0))
    return pl.pallas_call(
        _conformer_kernel,
        out_shape=out_shape,
        grid=(B, nt),
        in_specs=[tok] + [full(p) for p in params],
        out_specs=out_specs,
        scratch_shapes=[pltpu.VMEM((HALO + tt + SUBLANES, D), F32), pltpu.VMEM((8, LANES), F32)],
        compiler_params=pltpu.CompilerParams(
            dimension_semantics=("arbitrary", "arbitrary"), vmem_limit_bytes=VMEM_LIMIT),
        name="conformer",
    )(x, *params)


def _dispatch_kernel(pos_ref, pad_ref, xg_ref, xs_ref, zero_ref, sem, zsem):
    tt = xg_ref.shape[0]
    i = pl.program_id(0)
    base = i * tt
    n_seg = 2 * N_GROUPS

    def zero_fill(action):
        for s in range(n_seg):
            n = pl.multiple_of(pad_ref[n_seg + s], SUBLANES)
            start = pl.multiple_of(pad_ref[s], SUBLANES)
            copy = pltpu.make_async_copy(zero_ref.at[pl.ds(0, n)], xs_ref.at[pl.ds(start, n)], zsem)
            pl.when(n > 0)(getattr(copy, action))
        for g in range(N_GROUPS):
            start, n = pad_ref[2 * n_seg + g], pad_ref[2 * n_seg + N_GROUPS + g]
            for u in range(SUBLANES - 1):
                copy = pltpu.make_async_copy(zero_ref.at[pl.ds(0, 1)], xs_ref.at[pl.ds(start + u, 1)], zsem)
                pl.when(u < n)(getattr(copy, action))

    @pl.when(i == 0)
    def _():
        zero_ref[...] = jnp.zeros_like(zero_ref)
        zero_fill("start")

    def issue(jj, carry):
        j0 = pl.multiple_of(jj * SUBLANES, SUBLANES)
        for u in range(SUBLANES):
            src = xg_ref.at[pl.ds(j0, SUBLANES)].at[pl.ds(u, 1)]
            pltpu.make_async_copy(src, xs_ref.at[pl.ds(pos_ref[base + j0 + u], 1)], sem).start()
        return carry

    lax.fori_loop(0, tt // SUBLANES, issue, 0)
    pltpu.make_async_copy(xg_ref, xs_ref.at[pl.ds(0, tt)], sem).wait()

    @pl.when(i == 0)
    def _():
        zero_fill("wait")


def _dispatch(pos, pad, xg, n_rows, *, tt, tm):
    N, W = xg.shape
    return pl.pallas_call(
        _dispatch_kernel,
        out_shape=jax.ShapeDtypeStruct((n_rows, W), F32),
        grid_spec=pltpu.PrefetchScalarGridSpec(
            num_scalar_prefetch=2,
            grid=(N // tt,),
            in_specs=[pl.BlockSpec((tt, W), lambda i, pos, pad: (i, 0))],
            out_specs=pl.BlockSpec(memory_space=pl.ANY),
            scratch_shapes=[pltpu.VMEM((tm, W), F32), pltpu.SemaphoreType.DMA, pltpu.SemaphoreType.DMA]),
        compiler_params=pltpu.CompilerParams(
            dimension_semantics=("arbitrary",), vmem_limit_bytes=VMEM_LIMIT, disable_bounds_checks=True),
        name="moe_dispatch",
    )(pos, pad, xg)


def _moe_group_kernel(tg_ref, nu_ref, xs_ref, wg_ref, wu_ref, wd_ref, ys_ref, *, eb):
    i = pl.program_id(0)
    e = pl.program_id(1)
    d = ys_ref.shape[1]

    @pl.when(i < nu_ref[0])
    def _():
        x = xs_ref[:, :d].astype(BF16)
        gates = xs_ref[:, d:]
        lane = lax.broadcasted_iota(jnp.int32, gates.shape, 1)
        first = tg_ref[i] * EXPERTS_PER_GROUP + e * eb
        zs = []
        for q in range(eb):
            ge = jnp.sum(jnp.where(lane == first + q, gates, 0.0), axis=-1, keepdims=True)
            zg = _dot(x, wg_ref[q])
            zu = _dot(x, wu_ref[q])
            zs.append((zg * _sigmoid(zg) * zu * ge).astype(BF16))
        contrib = _dot(jnp.concatenate(zs, axis=-1), wd_ref[...].reshape(-1, d))

        @pl.when(e == 0)
        def _():
            ys_ref[...] = contrib

        @pl.when(e != 0)
        def _():
            ys_ref[...] += contrib

    @pl.when((i >= nu_ref[0]) & (e == 0))
    def _():
        ys_ref[...] = jnp.zeros_like(ys_ref)


def _moe_group(tile_group, n_used, xs, wg, wu, wd, *, layer, tm, eb):
    n_rows, W = xs.shape
    _, E, D, F = wg.shape
    steps = EXPERTS_PER_GROUP // eb
    row_map = lambda i, e, tg, nu: (jnp.minimum(i, nu[0] - 1), 0)
    w_map = lambda i, e, tg, nu: (layer, tg[i] * steps + e, 0, 0)
    return pl.pallas_call(
        functools.partial(_moe_group_kernel, eb=eb),
        out_shape=jax.ShapeDtypeStruct((n_rows, D), F32),
        grid_spec=pltpu.PrefetchScalarGridSpec(
            num_scalar_prefetch=2,
            grid=(n_rows // tm, steps),
            in_specs=[pl.BlockSpec((tm, W), row_map),
                      pl.BlockSpec((None, eb, D, F), w_map), pl.BlockSpec((None, eb, D, F), w_map),
                      pl.BlockSpec((None, eb, F, D), w_map)],
            out_specs=pl.BlockSpec((tm, D), lambda i, e, tg, nu: (i, 0))),
        compiler_params=pltpu.CompilerParams(
            dimension_semantics=("arbitrary", "arbitrary"), vmem_limit_bytes=VMEM_LIMIT),
        name="moe_group",
    )(tile_group, n_used, xs, wg, wu, wd)


def _combine_kernel(pos_ref, h1_ref, fg_ref, ys_ref, out_ref, buf_ref, sem, *, final_norm):
    tt = h1_ref.shape[0]
    i = pl.program_id(0)
    slot = lax.rem(i, 2)

    def gather(tile, s):
        base = tile * tt

        def issue(jj, carry):
            j0 = pl.multiple_of(jj * SUBLANES, SUBLANES)
            for u in range(SUBLANES):
                dst = buf_ref.at[s].at[pl.ds(j0, SUBLANES)].at[pl.ds(u, 1)]
                pltpu.make_async_copy(ys_ref.at[pl.ds(pos_ref[base + j0 + u], 1)], dst, sem.at[s]).start()
            return carry

        lax.fori_loop(0, tt // SUBLANES, issue, 0)

    @pl.when(i == 0)
    def _():
        gather(0, 0)

    @pl.when(i + 1 < pl.num_programs(0))
    def _():
        gather(i + 1, 1 - slot)

    pltpu.make_async_copy(ys_ref.at[pl.ds(0, tt)], buf_ref.at[slot], sem.at[slot]).wait()
    h = h1_ref[...] + buf_ref[slot]
    out_ref[...] = _rms(h, fg_ref[...]) if final_norm else h


def _combine(pos, h1, fg, ys, *, tt, final_norm):
    N, D = h1.shape
    tok = pl.BlockSpec((tt, D), lambda i, pos: (i, 0))
    return pl.pallas_call(
        functools.partial(_combine_kernel, final_norm=final_norm),
        out_shape=jax.ShapeDtypeStruct((N, D), F32),
        grid_spec=pltpu.PrefetchScalarGridSpec(
            num_scalar_prefetch=1,
            grid=(N // tt,),
            in_specs=[tok, pl.BlockSpec((1, D), lambda i, pos: (0, 0)), pl.BlockSpec(memory_space=pl.ANY)],
            out_specs=tok,
            scratch_shapes=[pltpu.VMEM((2, tt, D), F32), pltpu.SemaphoreType.DMA((2,))]),
        compiler_params=pltpu.CompilerParams(
            dimension_semantics=("arbitrary",), vmem_limit_bytes=VMEM_LIMIT, disable_bounds_checks=True),
        name="moe_combine_final" if final_norm else "moe_combine",
    )(pos, h1, fg, ys)


def _moe(route, counts, xg, h1, wg, wu, wd, fg, *, layer, tm, tt, eb, final_norm):
    N, D = h1.shape
    group = route[:, 0].astype(jnp.int32)
    rank = route[:, 1].astype(jnp.int32)
    cnt = counts[0, :N_GROUPS].astype(jnp.int32)
    padded = (cnt + tm - 1) // tm * tm
    ends = jnp.cumsum(padded)
    pos = (ends - padded)[group] + rank
    n_tiles = N // tm + N_GROUPS
    n_used = ends[-1:] // tm
    tile_group = jnp.sum((jnp.arange(n_tiles)[:, None] * tm >= ends[None, :]).astype(jnp.int32), axis=1)
    tile_group = jnp.minimum(tile_group, tile_group[n_used[0] - 1])
    tail = ends[-1] + jnp.arange(N_GROUPS) * tm
    pad_start, pad_rows = ends - padded + cnt, padded - cnt
    head_rows = jnp.minimum(pad_rows, (-pad_start) % SUBLANES)
    pad = jnp.concatenate([pad_start + head_rows, tail, pad_rows - head_rows, jnp.where(tail < n_tiles * tm, tm, 0),
                           pad_start, head_rows]).astype(jnp.int32)
    xs = _dispatch(pos, pad, xg, n_tiles * tm, tt=tt, tm=tm)
    ys = _moe_group(tile_group, n_used, xs, wg, wu, wd, layer=layer, tm=tm, eb=eb)
    return _combine(pos, h1, fg, ys, tt=tt, final_norm=final_norm)


def _router_weights(w_group, b_group, w_expert, b_expert):
    D = w_group.shape[0]
    w = jnp.zeros((D, LANES), F32).at[:, :N_EXPERTS].set(w_expert).at[:, N_EXPERTS:N_EXPERTS + N_GROUPS].set(w_group)
    b = jnp.zeros((1, LANES), F32).at[0, :N_EXPERTS].set(b_expert).at[0, N_EXPERTS:N_EXPERTS + N_GROUPS].set(b_group)
    hi = w.astype(BF16)
    lo = (w - hi.astype(F32)).astype(BF16)
    return hi, lo, b


def _pad_cols(w, n):
    return jnp.pad(w, ((0, 0), (0, n - w.shape[1])))


def _pad_rows(w, n):
    return jnp.pad(w, ((0, n - w.shape[0]), (0, 0)))


def _tile(n, want):
    t = min(n, want)
    while n % t:
        t //= 2
    return t


def kernel(x, norm1_g, norm2_g, final_g, rwkv_mu, rwkv_w_rkv, rwkv_w0, rwkv_w1, rwkv_w2, rwkv_a0, rwkv_a1, rwkv_a2, rwkv_g1, rwkv_g2, rwkv_k_k, rwkv_k_a, rwkv_r_k, rwkv_ln_g, rwkv_ln_b, rwkv_w_o, conv_w_pw1, conv_b_pw1, conv_w_dw, conv_b_dw, conv_ln_g, conv_ln_b, conv_w_pw2, conv_b_pw2, moe_w_group, moe_b_group, moe_w_expert, moe_b_expert, moe_w_gate, moe_w_up, moe_w_down):
    B, T, D = x.shape
    N = B * T
    depth = norm1_g.shape[0]
    assert depth == 2 and D % QUAD == 0 and T % CHUNK == 0
    row = lambda v: v.reshape(1, -1)
    bf = lambda w: w.astype(BF16)

    r, k, v, wl, a, g = _rwkv_pre(
        x, row(norm1_g[0]), rwkv_mu[0], bf(rwkv_w_rkv[0]), row(rwkv_w0[0]),
        bf(_pad_cols(rwkv_w1[0], LANES)), bf(_pad_rows(rwkv_w2[0], LANES)), row(rwkv_a0[0]),
        bf(_pad_cols(rwkv_a1[0], LANES)), bf(_pad_rows(rwkv_a2[0], LANES)),
        bf(rwkv_g1[0]), bf(rwkv_g2[0]), tt=_tile(T, 512))
    y = _recurrence(r, k, v, wl, a, row(rwkv_k_k[0]), row(rwkv_k_a[0]), rwkv_r_k[0].reshape(1, -1),
                    row(rwkv_ln_g[0]), row(rwkv_ln_b[0]))
    wr_hi, wr_lo, br = _router_weights(moe_w_group[0], moe_b_group[0], moe_w_expert[0], moe_b_expert[0])
    h1, xg, route, counts = _rwkv_post(y.reshape(N, D), g.reshape(N, D), x.reshape(N, D), bf(rwkv_w_o[0]),
                                       row(norm2_g[0]), wr_hi, wr_lo, br, tt=_tile(N, 512))
    moe_tiles = dict(tm=_tile(N, 1024), tt=_tile(N, 512), eb=4)
    moe_w = (bf(moe_w_gate), bf(moe_w_up), bf(moe_w_down))
    h = _moe(route, counts, xg, h1, *moe_w, row(final_g),
             layer=0, final_norm=False, **moe_tiles)

    wr_hi, wr_lo, br = _router_weights(moe_w_group[1], moe_b_group[1], moe_w_expert[1], moe_b_expert[1])
    h1, xg, route, counts = _conformer(
        h.reshape(B, T, D), row(norm1_g[1]), bf(conv_w_pw1[0]), row(conv_b_pw1[0]), conv_w_dw[0], row(conv_b_dw[0]),
        row(conv_ln_g[0]), row(conv_ln_b[0]), bf(conv_w_pw2[0]), row(conv_b_pw2[0]),
        row(norm2_g[1]), wr_hi, wr_lo, br, tt=_tile(T, 256))
    out = _moe(route, counts, xg, h1, *moe_w, row(final_g),
               layer=1, final_norm=True, **moe_tiles)
    return out.reshape(B, T, D)
```

```python
import functools
import math

import jax
import jax.numpy as jnp
from jax import lax
from jax.experimental import pallas as pl
from jax.experimental.pallas import tpu as pltpu

F32 = jnp.float32
BF16 = jnp.bfloat16

HEAD = 64
LNX_EPS = HEAD * 1e-5
CONV_WIDTH = 31
CONV_LN_EPS = 1e-5
N_GROUPS = 4
EXPERTS_PER_GROUP = 8
N_EXPERTS = N_GROUPS * EXPERTS_PER_GROUP
RMS_EPS = 1e-6

LANES = 128
SUBLANES = 8
QUAD = 4 * HEAD
CHUNK = 64
VMEM_LIMIT = 56 * 1024 * 1024

NEG_BIG = -1e30


def _dot(a, b):
    return jnp.dot(a, b, preferred_element_type=F32)


def _dot_nt(a, b):
    return lax.dot_general(a, b, (((1,), (1,)), ((), ())), preferred_element_type=F32)


def _dot_tn(a, b):
    return lax.dot_general(a, b, (((0,), (0,)), ((), ())), preferred_element_type=F32)


def _split2(x):
    hi = x.astype(BF16)
    lo = (x - hi.astype(F32)).astype(BF16)
    return hi, lo


def _sigmoid(x):
    return 1.0 / (1.0 + jnp.exp(-x))


def _rms(x, g):
    return x * lax.rsqrt(jnp.mean(x * x, axis=-1, keepdims=True) + RMS_EPS) * g


def _rwkv_pre_kernel(x_ref, n1g_ref, mu_ref, wrkv_ref, w0_ref, w1_ref, w2_ref, a0_ref, a1_ref, a2_ref,
                     g1_ref, g2_ref, r_ref, k_ref, v_ref, wl_ref, a_ref, g_ref, prev_ref):
    i = pl.program_id(1)
    tt = x_ref.shape[0]
    hn = _rms(x_ref[...], n1g_ref[...])

    @pl.when(i == 0)
    def _():
        prev_ref[...] = jnp.zeros_like(prev_ref)

    prev = prev_ref[0:1, :]
    rolled = pltpu.roll(hn, 1, 0)
    row = lax.broadcasted_iota(jnp.int32, hn.shape, 0)
    shifted = jnp.where(row == 0, prev, rolled)
    prev_ref[0:1, :] = hn[tt - 1:tt, :]
    xx = shifted - hn

    def mix(j):
        return (hn + xx * mu_ref[j:j + 1, :]).astype(BF16)

    r_ref[...] = _dot(mix(0), wrkv_ref[0]).astype(BF16)
    k_ref[...] = _dot(mix(2), wrkv_ref[1]).astype(BF16)
    v_ref[...] = _dot(mix(3), wrkv_ref[2]).astype(BF16)
    zw = w0_ref[...] + _dot(jnp.tanh(_dot(mix(1), w1_ref[...])).astype(BF16), w2_ref[...])
    wl_ref[...] = -math.exp(-0.5) * _sigmoid(zw)
    a = _sigmoid(a0_ref[...] + _dot(_dot(mix(4), a1_ref[...]).astype(BF16), a2_ref[...]))
    a_ref[...] = a.astype(BF16)
    g_ref[...] = _dot(_sigmoid(_dot(mix(5), g1_ref[...])).astype(BF16), g2_ref[...]).astype(BF16)


def _rwkv_pre(x, n1g, mu, wrkv, w0, w1, w2, a0, a1, a2, g1, g2, *, tt):
    B, T, D = x.shape
    tok = pl.BlockSpec((None, tt, D), lambda b, i: (b, i, 0))

    def full(arr):
        nd = arr.ndim
        return pl.BlockSpec(arr.shape, lambda b, i: (0,) * nd)

    params = (n1g, mu, wrkv, w0, w1, w2, a0, a1, a2, g1, g2)
    out = lambda dt: jax.ShapeDtypeStruct((B, T, D), dt)
    return pl.pallas_call(
        _rwkv_pre_kernel,
        out_shape=(out(BF16), out(BF16), out(BF16), out(F32), out(BF16), out(BF16)),
        grid=(B, T // tt),
        in_specs=[tok] + [full(p) for p in params],
        out_specs=(tok,) * 6,
        scratch_shapes=[pltpu.VMEM((8, D), F32)],
        compiler_params=pltpu.CompilerParams(
            dimension_semantics=("arbitrary", "arbitrary"), vmem_limit_bytes=VMEM_LIMIT),
        name="rwkv_pre",
    )(x, *params)


def _rec_consts(tb):
    L, Q = CHUNK, QUAD
    ri = lax.broadcasted_iota(jnp.int32, (Q, Q), 0)
    ci = lax.broadcasted_iota(jnp.int32, (Q, Q), 1)
    bd = (jnp.right_shift(ri, 6) == jnp.right_shift(ci, 6)).astype(F32)
    ei = lax.broadcasted_iota(jnp.int32, (L, 4 * L), 0)
    ej = lax.broadcasted_iota(jnp.int32, (L, 4 * L), 1)
    eye_lp = (jnp.bitwise_and(ej, L - 1) == ei).astype(F32)
    mi = lax.broadcasted_iota(jnp.int32, (2 * L, 8 * L), 0)
    mj = lax.broadcasted_iota(jnp.int32, (2 * L, 8 * L), 1)
    t_row = jnp.bitwise_and(mi, L - 1)
    t_col = jnp.bitwise_and(mj, L - 1)
    score_mask = ((t_col < t_row) | ((mi >= L) & (t_col == t_row))).astype(F32)
    bi = lax.broadcasted_iota(jnp.int32, (tb, tb), 0)
    bj = lax.broadcasted_iota(jnp.int32, (tb, tb), 1)
    tri_bd = ((jnp.right_shift(bi, 6) == jnp.right_shift(bj, 6)) & (bj <= bi)).astype(BF16)
    return bd, eye_lp, score_mask, tri_bd


def _recurrence_kernel(r_ref, k_ref, v_ref, wl_ref, a_ref, kk_ref, ka_ref, rk_ref, lg_ref, lb_ref,
                       y_ref, z_ref, rt_s, at_s, kt_s, bt_s, kh_s, bh_s, vb_s, w_s, arb_s, u0_s, y0_s, bon_s,
                       kbt_s, gcol_s):
    assert CHUNK == HEAD and QUAD == 4 * CHUNK
    L, Q = CHUNK, QUAD
    t = pl.program_id(0)
    nc, _, width = r_ref.shape
    tb = nc * L
    nq = width // Q

    def load(ref, cols):
        return jnp.concatenate([ref[b, :, cols].astype(F32) for b in range(nc)], axis=0)
    bd_f, eye_lp, score_mask_f, tri_bd = _rec_consts(tb)
    bd_b = bd_f.astype(BF16)

    def bdsum(x, passes=1):
        if passes == 1:
            return _dot(x.astype(BF16), bd_b)
        hi, lo = _split2(x)
        return _dot(hi, bd_b) + _dot(lo, bd_b)

    def stack(xb):
        return jnp.concatenate([xb] * 4, axis=0) * bd_b

    @pl.when(t == 0)
    def _():
        z_ref[...] = jnp.zeros_like(z_ref)

    for q in range(nq):
        cols = slice(q * Q, (q + 1) * Q)
        r, k, v, wl, a = (load(ref, cols) for ref in (r_ref, k_ref, v_ref, wl_ref, a_ref))
        kk = k * kk_ref[:, cols]
        kk = kk * lax.rsqrt(jnp.maximum(bdsum(kk * kk, passes=2), 1e-24))
        k2 = k * (1.0 + (a - 1.0) * ka_ref[:, cols])
        bb = kk * a
        bon_s[:, cols] = bdsum(r * k2 * rk_ref[:, cols]) * v
        w_hi, w_lo = _split2(wl)
        cw = _dot(tri_bd, w_hi) + _dot(tri_bd, w_lo)
        rt_s[:, cols] = (r * jnp.exp(cw)).astype(BF16)
        at_s[:, cols] = (-kk * jnp.exp(cw - wl)).astype(BF16)
        g_inv = jnp.exp(-cw)
        kt_s[:, cols] = (k2 * g_inv).astype(BF16)
        bt_s[:, cols] = (bb * g_inv).astype(BF16)
        vb_s[:, cols] = v.astype(BF16)
        for c in range(nc):
            rows = slice(c * L, (c + 1) * L)
            tot = cw[(c + 1) * L - 1:(c + 1) * L, :]
            g_rem = jnp.exp(tot - cw[rows, :])
            kh_s[rows, cols] = (k2[rows, :] * g_rem).astype(BF16)
            bh_s[rows, cols] = (bb[rows, :] * g_rem).astype(BF16)
            gcol_s[c * nq + q] = jnp.broadcast_to(jnp.transpose(jnp.exp(tot)), (Q, LANES))

    tiles = [(slice(c * L, (c + 1) * L), slice(q * Q, (q + 1) * Q), c * nq + q)
             for c in range(nc) for q in range(nq)]
    score_mask = score_mask_f > 0.5
    for rows, cols, j in tiles:
        rhs = jnp.concatenate([stack(kt_s[rows, cols]), stack(bt_s[rows, cols])], axis=0)
        lhs = jnp.concatenate([at_s[rows, cols], rt_s[rows, cols]], axis=0)
        sc = jnp.where(score_mask, _dot_nt(lhs, rhs), 0.0)
        kt_s[rows, cols] = sc[:L, :4 * L].astype(BF16)
        bt_s[rows, cols] = sc[L:, :4 * L].astype(BF16)
        arb_s[rows, cols] = sc[L:, 4 * L:].astype(BF16)
        a_ab = sc[:L, 4 * L:]
        w_s[rows, cols] = a_ab.astype(BF16)
        u0_s[rows, cols] = eye_lp + a_ab
        khbh = jnp.concatenate([kh_s[rows, cols], bh_s[rows, cols]], axis=0).astype(F32)
        kbt_s[j] = jnp.transpose(khbh).astype(BF16)
    for rows, cols, j in tiles:
        nb = w_s[rows, cols]
        w_s[rows, cols] = _dot(nb, stack(nb)).astype(BF16)
    for _ in range(int(math.log2(L)) - 2):
        for rows, cols, j in tiles:
            mb = w_s[rows, cols]
            tinv = u0_s[rows, cols]
            res = _dot(jnp.concatenate([mb, tinv.astype(BF16)], axis=0), stack(mb))
            w_s[rows, cols] = res[:L].astype(BF16)
            u0_s[rows, cols] = tinv + res[L:]
    for rows, cols, j in tiles:
        lhs = jnp.concatenate([kt_s[rows, cols], bt_s[rows, cols]], axis=0)
        pu_y0 = _dot(lhs, stack(vb_s[rows, cols]))
        kt_s[rows, cols] = pu_y0[:L].astype(BF16)
        y0_s[rows, cols] = pu_y0[L:]
    for rows, cols, j in tiles:
        tinv = u0_s[rows, cols]
        tinv = tinv + _dot(tinv.astype(BF16), stack(w_s[rows, cols]))
        wu = _dot(tinv.astype(BF16),
                  jnp.concatenate([stack(at_s[rows, cols]), stack(kt_s[rows, cols])], axis=1))
        w_s[rows, cols] = wu[:, :Q].astype(BF16)
        u0_s[rows, cols] = wu[:, Q:]

    for rows, cols, j in tiles:
        z = z_ref[j]
        wr = _dot(jnp.concatenate([w_s[rows, cols], rt_s[rows, cols]], axis=0), z.astype(BF16))
        usb = (wr[:L] + u0_s[rows, cols]).astype(BF16)
        y0_s[rows, cols] = y0_s[rows, cols] + wr[L:] + _dot(arb_s[rows, cols], stack(usb))
        upd = _dot(kbt_s[j], jnp.concatenate([vb_s[rows, cols], usb], axis=0))
        g = gcol_s[j]
        z_ref[j] = z * jnp.concatenate([g, g], axis=1) + upd * bd_f

    for q in range(nq):
        cols = slice(q * Q, (q + 1) * Q)
        y = y0_s[:, cols]
        yc = y - bdsum(y) * (1.0 / HEAD)
        var = bdsum(yc * yc) * (1.0 / HEAD)
        out = bon_s[:, cols] + yc * lax.rsqrt(var + LNX_EPS) * lg_ref[:, cols] + lb_ref[:, cols]
        for b in range(nc):
            y_ref[b, :, cols] = out[b * L:(b + 1) * L, :].astype(y_ref.dtype)


def _recurrence(r, k, v, wl, a, k_k, k_a, r_k, ln_g, ln_b):
    B, T, D = r.shape
    tok = pl.BlockSpec((B, CHUNK, D), lambda t: (0, t, 0))
    par = pl.BlockSpec((1, D), lambda t: (0, 0))
    n_tiles = B * (D // QUAD)
    blk_bf = pltpu.VMEM((B * CHUNK, D), BF16)
    blk_f32 = pltpu.VMEM((B * CHUNK, D), F32)
    return pl.pallas_call(
        _recurrence_kernel,
        out_shape=jax.ShapeDtypeStruct((B, T, D), BF16),
        grid=(T // CHUNK,),
        in_specs=[tok] * 5 + [par] * 5,
        out_specs=tok,
        scratch_shapes=[pltpu.VMEM((n_tiles, QUAD, QUAD), F32)] + [blk_bf] * 9 + [blk_f32] * 3
        + [pltpu.VMEM((n_tiles, QUAD, 2 * CHUNK), BF16), pltpu.VMEM((n_tiles, QUAD, LANES), F32)],
        compiler_params=pltpu.CompilerParams(
            dimension_semantics=("arbitrary",), vmem_limit_bytes=VMEM_LIMIT),
        name="rwkv_recurrence",
    )(r, k, v, wl, a, k_k, k_a, r_k, ln_g, ln_b)


def _route(h1, n2g, wr_hi, wr_lo, br, xg_ref, route_ref, counts_ref, cnt_ref, is_first):
    d = h1.shape[1]
    hn = _rms(h1, n2g)
    x_hi, x_lo = _split2(hn)
    logits = _dot(x_hi, wr_hi) + _dot(x_lo, wr_hi) + _dot(x_hi, wr_lo) + br
    lane = lax.broadcasted_iota(jnp.int32, logits.shape, 1)

    def first_argmax(vals):
        mx = jnp.max(vals, axis=-1, keepdims=True)
        idx = jnp.min(jnp.where(vals == mx, lane, 4 * LANES), axis=-1, keepdims=True)
        return mx, idx

    lg = jnp.where((lane >= N_EXPERTS) & (lane < N_EXPERTS + N_GROUPS), logits, NEG_BIG)
    g_max, g_lane = first_argmax(lg)
    p_top = 1.0 / jnp.sum(jnp.exp(lg - g_max), axis=-1, keepdims=True)
    g_idx = g_lane - N_EXPERTS
    in_group = (lane >= g_idx * EXPERTS_PER_GROUP) & (lane < (g_idx + 1) * EXPERTS_PER_GROUP)
    le = jnp.where(in_group, logits, NEG_BIG)
    m1, i1 = first_argmax(le)
    m2, i2 = first_argmax(jnp.where(lane == i1, NEG_BIG, le))
    e2 = jnp.exp(m2 - m1)
    w1 = p_top / (1.0 + e2)
    w2 = p_top * e2 / (1.0 + e2)
    gates = jnp.where(lane == i1, w1, jnp.where(lane == i2, w2, 0.0))
    xg_ref[:, :d] = hn
    xg_ref[:, d:] = gates

    @pl.when(is_first)
    def _():
        cnt_ref[...] = jnp.zeros_like(cnt_ref)

    rows = h1.shape[0]
    onehot = lane == g_idx
    ri = lax.broadcasted_iota(jnp.int32, (rows, rows), 0)
    ci = lax.broadcasted_iota(jnp.int32, (rows, rows), 1)
    before = _dot((ci < ri).astype(F32).astype(BF16), onehot.astype(F32).astype(BF16)) + cnt_ref[0:1, :]
    rank = jnp.sum(jnp.where(onehot, before, 0.0), axis=-1, keepdims=True)
    route_ref[...] = jnp.where(lane == 0, g_idx.astype(F32), jnp.where(lane == 1, rank, 0.0))
    total = cnt_ref[0:1, :] + jnp.sum(onehot.astype(F32), axis=0, keepdims=True)
    cnt_ref[0:1, :] = total
    counts_ref[...] = total


def _rwkv_post_kernel(y_ref, g_ref, x_ref, wo_ref, n2g_ref, wrh_ref, wrl_ref, br_ref,
                      h1_ref, xg_ref, route_ref, counts_ref, cnt_ref):
    yg = (y_ref[...] * g_ref[...]).astype(BF16)
    h1 = x_ref[...] + _dot(yg, wo_ref[...])
    h1_ref[...] = h1
    _route(h1, n2g_ref[...], wrh_ref[...], wrl_ref[...], br_ref[...], xg_ref, route_ref, counts_ref, cnt_ref,
           pl.program_id(0) == 0)


def _tail_outputs(N, D, tt, tok_map):
    shapes = (jax.ShapeDtypeStruct((N, D), F32), jax.ShapeDtypeStruct((N, D + LANES), F32),
              jax.ShapeDtypeStruct((N, LANES), F32), jax.ShapeDtypeStruct((1, LANES), F32))
    specs = (pl.BlockSpec((tt, D), tok_map), pl.BlockSpec((tt, D + LANES), tok_map),
             pl.BlockSpec((tt, LANES), tok_map), pl.BlockSpec((1, LANES), lambda *_: (0, 0)))
    return shapes, specs


def _rwkv_post(y, g, x, wo, n2g, wr_hi, wr_lo, br, *, tt):
    N, D = x.shape
    tok_map = lambda i: (i, 0)
    tok = pl.BlockSpec((tt, D), tok_map)

    def full(arr):
        nd = arr.ndim
        return pl.BlockSpec(arr.shape, lambda i: (0,) * nd)

    params = (wo, n2g, wr_hi, wr_lo, br)
    out_shape, out_specs = _tail_outputs(N, D, tt, tok_map)
    return pl.pallas_call(
        _rwkv_post_kernel,
        out_shape=out_shape,
        grid=(N // tt,),
        in_specs=[tok, tok, tok] + [full(p) for p in params],
        out_specs=out_specs,
        scratch_shapes=[pltpu.VMEM((8, LANES), F32)],
        compiler_params=pltpu.CompilerParams(
            dimension_semantics=("arbitrary",), vmem_limit_bytes=VMEM_LIMIT),
        name="rwkv_post",
    )(y, g, x, *params)


HALO = 32


def _conformer_kernel(x_ref, n1g_ref, wpw1_ref, bpw1_ref, wdw_ref, bdw_ref, lng_ref, lnb_ref, wpw2_ref, bpw2_ref,
                      n2g_ref, wrh_ref, wrl_ref, br_ref, h1_ref, xg_ref, route_ref, counts_ref, ubuf_ref, cnt_ref):
    i = pl.program_id(1)
    tt, D = x_ref.shape
    x = x_ref[...]
    hn = _rms(x, n1g_ref[...]).astype(BF16)
    u = _dot(hn, wpw1_ref[...]) + bpw1_ref[...]
    u = u[:, :D] * _sigmoid(u[:, D:])

    @pl.when(i == 0)
    def _():
        ubuf_ref[0:HALO, :] = jnp.zeros((HALO, D), F32)
        ubuf_ref[HALO + tt:, :] = jnp.zeros((SUBLANES, D), F32)

    ubuf_ref[HALO:HALO + tt, :] = u
    base = HALO - (CONV_WIDTH - 1)
    acc = jnp.zeros((tt, D), F32) + bdw_ref[...]
    for s in range(SUBLANES):
        part = None
        for q in range((base + CONV_WIDTH - 1) // SUBLANES + 1):
            j = q * SUBLANES + s - base
            if 0 <= j < CONV_WIDTH:
                term = wdw_ref[j:j + 1, :] * ubuf_ref[q * SUBLANES:q * SUBLANES + tt + SUBLANES, :]
                part = term if part is None else part + term
        acc = acc + part[s:s + tt, :]
    ubuf_ref[0:HALO, :] = ubuf_ref[tt:tt + HALO, :]

    m = jnp.mean(acc, axis=-1, keepdims=True)
    c = acc - m
    var = jnp.mean(c * c, axis=-1, keepdims=True)
    c = c * lax.rsqrt(var + CONV_LN_EPS) * lng_ref[...] + lnb_ref[...]
    c = c * _sigmoid(c)
    h1 = x + _dot(c.astype(BF16), wpw2_ref[...]) + bpw2_ref[...]
    h1_ref[...] = h1
    _route(h1, n2g_ref[...], wrh_ref[...], wrl_ref[...], br_ref[...], xg_ref, route_ref, counts_ref, cnt_ref,
           (pl.program_id(0) == 0) & (i == 0))


def _conformer(x, n1g, wpw1, bpw1, wdw, bdw, lng, lnb, wpw2, bpw2, n2g, wr_hi, wr_lo, br, *, tt):
    B, T, D = x.shape
    nt = T // tt
    tok = pl.BlockSpec((None, tt, D), lambda b, i: (b, i, 0))

    def full(arr):
        nd = arr.ndim
        return pl.BlockSpec(arr.shape, lambda b, i: (0,) * nd)

    params = (n1g, wpw1, bpw1, wdw, bdw, lng, lnb, wpw2, bpw2, n2g, wr_hi, wr_lo, br)
    out_shape, out_specs = _tail_outputs(B * T, D, tt, lambda b, i: (b * nt + i, 0))
    return pl.pallas_call(
        _conformer_kernel,
        out_shape=out_shape,
        grid=(B, nt),
        in_specs=[tok] + [full(p) for p in params],
        out_specs=out_specs,
        scratch_shapes=[pltpu.VMEM((HALO + tt + SUBLANES, D), F32), pltpu.VMEM((8, LANES), F32)],
        compiler_params=pltpu.CompilerParams(
            dimension_semantics=("arbitrary", "arbitrary"), vmem_limit_bytes=VMEM_LIMIT),
        name="conformer",
    )(x, *params)


def _dispatch_kernel(pos_ref, pad_ref, xg_ref, xs_ref, zero_ref, sem, zsem):
    tt = xg_ref.shape[0]
    i = pl.program_id(0)
    base = i * tt
    n_seg = 2 * N_GROUPS

    def zero_fill(action):
        for s in range(n_seg):
            n = pl.multiple_of(pad_ref[n_seg + s], SUBLANES)
            start = pl.multiple_of(pad_ref[s], SUBLANES)
            copy = pltpu.make_async_copy(zero_ref.at[pl.ds(0, n)], xs_ref.at[pl.ds(start, n)], zsem)
            pl.when(n > 0)(getattr(copy, action))
        for g in range(N_GROUPS):
            start, n = pad_ref[2 * n_seg + g], pad_ref[2 * n_seg + N_GROUPS + g]
            for u in range(SUBLANES - 1):
                copy = pltpu.make_async_copy(zero_ref.at[pl.ds(0, 1)], xs_ref.at[pl.ds(start + u, 1)], zsem)
                pl.when(u < n)(getattr(copy, action))

    @pl.when(i == 0)
    def _():
        zero_ref[...] = jnp.zeros_like(zero_ref)
        zero_fill("start")

    def issue(jj, carry):
        j0 = pl.multiple_of(jj * SUBLANES, SUBLANES)
        for u in range(SUBLANES):
            src = xg_ref.at[pl.ds(j0, SUBLANES)].at[pl.ds(u, 1)]
            pltpu.make_async_copy(src, xs_ref.at[pl.ds(pos_ref[base + j0 + u], 1)], sem).start()
        return carry

    lax.fori_loop(0, tt // SUBLANES, issue, 0)
    pltpu.make_async_copy(xg_ref, xs_ref.at[pl.ds(0, tt)], sem).wait()

    @pl.when(i == 0)
    def _():
        zero_fill("wait")


def _dispatch(pos, pad, xg, n_rows, *, tt, tm):
    N, W = xg.shape
    return pl.pallas_call(
        _dispatch_kernel,
        out_shape=jax.ShapeDtypeStruct((n_rows, W), F32),
        grid_spec=pltpu.PrefetchScalarGridSpec(
            num_scalar_prefetch=2,
            grid=(N // tt,),
            in_specs=[pl.BlockSpec((tt, W), lambda i, pos, pad: (i, 0))],
            out_specs=pl.BlockSpec(memory_space=pl.ANY),
            scratch_shapes=[pltpu.VMEM((tm, W), F32), pltpu.SemaphoreType.DMA, pltpu.SemaphoreType.DMA]),
        compiler_params=pltpu.CompilerParams(
            dimension_semantics=("arbitrary",), vmem_limit_bytes=VMEM_LIMIT, disable_bounds_checks=True),
        name="moe_dispatch",
    )(pos, pad, xg)


def _moe_group_kernel(tg_ref, nu_ref, xs_ref, wg_ref, wu_ref, wd_ref, ys_ref, *, eb):
    i = pl.program_id(0)
    e = pl.program_id(1)
    d = ys_ref.shape[1]

    @pl.when(i < nu_ref[0])
    def _():
        x = xs_ref[:, :d].astype(BF16)
        gates = xs_ref[:, d:]
        lane = lax.broadcasted_iota(jnp.int32, gates.shape, 1)
        first = tg_ref[i] * EXPERTS_PER_GROUP + e * eb
        zs = []
        for q in range(eb):
            ge = jnp.sum(jnp.where(lane == first + q, gates, 0.0), axis=-1, keepdims=True)
            zg = _dot(x, wg_ref[q])
            zu = _dot(x, wu_ref[q])
            zs.append((zg * _sigmoid(zg) * zu * ge).astype(BF16))
        contrib = _dot(jnp.concatenate(zs, axis=-1), wd_ref[...].reshape(-1, d))

        @pl.when(e == 0)
        def _():
            ys_ref[...] = contrib

        @pl.when(e != 0)
        def _():
            ys_ref[...] += contrib

    @pl.when((i >= nu_ref[0]) & (e == 0))
    def _():
        ys_ref[...] = jnp.zeros_like(ys_ref)


def _moe_group(tile_group, n_used, xs, wg, wu, wd, *, layer, tm, eb):
    n_rows, W = xs.shape
    _, E, D, F = wg.shape
    steps = EXPERTS_PER_GROUP // eb
    row_map = lambda i, e, tg, nu: (jnp.minimum(i, nu[0] - 1), 0)
    w_map = lambda i, e, tg, nu: (layer, tg[i] * steps + e, 0, 0)
    return pl.pallas_call(
        functools.partial(_moe_group_kernel, eb=eb),
        out_shape=jax.ShapeDtypeStruct((n_rows, D), F32),
        grid_spec=pltpu.PrefetchScalarGridSpec(
            num_scalar_prefetch=2,
            grid=(n_rows // tm, steps),
            in_specs=[pl.BlockSpec((tm, W), row_map),
                      pl.BlockSpec((None, eb, D, F), w_map), pl.BlockSpec((None, eb, D, F), w_map),
                      pl.BlockSpec((None, eb, F, D), w_map)],
            out_specs=pl.BlockSpec((tm, D), lambda i, e, tg, nu: (i, 0))),
        compiler_params=pltpu.CompilerParams(
            dimension_semantics=("arbitrary", "arbitrary"), vmem_limit_bytes=VMEM_LIMIT),
        name="moe_group",
    )(tile_group, n_used, xs, wg, wu, wd)


def _combine_kernel(pos_ref, h1_ref, fg_ref, ys_ref, out_ref, buf_ref, sem, *, final_norm):
    tt = h1_ref.shape[0]
    i = pl.program_id(0)
    slot = lax.rem(i, 2)

    def gather(tile, s):
        base = tile * tt

        def issue(jj, carry):
            j0 = pl.multiple_of(jj * SUBLANES, SUBLANES)
            for u in range(SUBLANES):
                dst = buf_ref.at[s].at[pl.ds(j0, SUBLANES)].at[pl.ds(u, 1)]
                pltpu.make_async_copy(ys_ref.at[pl.ds(pos_ref[base + j0 + u], 1)], dst, sem.at[s]).start()
            return carry

        lax.fori_loop(0, tt // SUBLANES, issue, 0)

    @pl.when(i == 0)
    def _():
        gather(0, 0)

    @pl.when(i + 1 < pl.num_programs(0))
    def _():
        gather(i + 1, 1 - slot)

    pltpu.make_async_copy(ys_ref.at[pl.ds(0, tt)], buf_ref.at[slot], sem.at[slot]).wait()
    h = h1_ref[...] + buf_ref[slot]
    out_ref[...] = _rms(h, fg_ref[...]) if final_norm else h


def _combine(pos, h1, fg, ys, *, tt, final_norm):
    N, D = h1.shape
    tok = pl.BlockSpec((tt, D), lambda i, pos: (i, 0))
    return pl.pallas_call(
        functools.partial(_combine_kernel, final_norm=final_norm),
        out_shape=jax.ShapeDtypeStruct((N, D), F32),
        grid_spec=pltpu.PrefetchScalarGridSpec(
            num_scalar_prefetch=1,
            grid=(N // tt,),
            in_specs=[tok, pl.BlockSpec((1, D), lambda i, pos: (0, 0)), pl.BlockSpec(memory_space=pl.ANY)],
            out_specs=tok,
            scratch_shapes=[pltpu.VMEM((2, tt, D), F32), pltpu.SemaphoreType.DMA((2,))]),
        compiler_params=pltpu.CompilerParams(
            dimension_semantics=("arbitrary",), vmem_limit_bytes=VMEM_LIMIT, disable_bounds_checks=True),
        name="moe_combine_final" if final_norm else "moe_combine",
    )(pos, h1, fg, ys)


def _moe(route, counts, xg, h1, wg, wu, wd, fg, *, layer, tm, tt, eb, final_norm):
    N, D = h1.shape
    group = route[:, 0].astype(jnp.int32)
    rank = route[:, 1].astype(jnp.int32)
    cnt = counts[0, :N_GROUPS].astype(jnp.int32)
    padded = (cnt + tm - 1) // tm * tm
    ends = jnp.cumsum(padded)
    pos = (ends - padded)[group] + rank
    n_tiles = N // tm + N_GROUPS
    n_used = ends[-1:] // tm
    tile_group = jnp.sum((jnp.arange(n_tiles)[:, None] * tm >= ends[None, :]).astype(jnp.int32), axis=1)
    tile_group = jnp.minimum(tile_group, tile_group[n_used[0] - 1])
    tail = ends[-1] + jnp.arange(N_GROUPS) * tm
    pad_start, pad_rows = ends - padded + cnt, padded - cnt
    head_rows = jnp.minimum(pad_rows, (-pad_start) % SUBLANES)
    pad = jnp.concatenate([pad_start + head_rows, tail, pad_rows - head_rows, jnp.where(tail < n_tiles * tm, tm, 0),
                           pad_start, head_rows]).astype(jnp.int32)
    xs = _dispatch(pos, pad, xg, n_tiles * tm, tt=tt, tm=tm)
    ys = _moe_group(tile_group, n_used, xs, wg, wu, wd, layer=layer, tm=tm, eb=eb)
    return _combine(pos, h1, fg, ys, tt=tt, final_norm=final_norm)


def _router_weights(w_group, b_group, w_expert, b_expert):
    D = w_group.shape[0]
    w = jnp.zeros((D, LANES), F32).at[:, :N_EXPERTS].set(w_expert).at[:, N_EXPERTS:N_EXPERTS + N_GROUPS].set(w_group)
    b = jnp.zeros((1, LANES), F32).at[0, :N_EXPERTS].set(b_expert).at[0, N_EXPERTS:N_EXPERTS + N_GROUPS].set(b_group)
    hi = w.astype(BF16)
    lo = (w - hi.astype(F32)).astype(BF16)
    return hi, lo, b


def _pad_cols(w, n):
    return jnp.pad(w, ((0, 0), (0, n - w.shape[1])))


def _pad_rows(w, n):
    return jnp.pad(w, ((0, n - w.shape[0]), (0, 0)))


def _tile(n, want):
    t = min(n, want)
    while n % t:
        t //= 2
    return t


def kernel(x, norm1_g, norm2_g, final_g, rwkv_mu, rwkv_w_rkv, rwkv_w0, rwkv_w1, rwkv_w2, rwkv_a0, rwkv_a1, rwkv_a2, rwkv_g1, rwkv_g2, rwkv_k_k, rwkv_k_a, rwkv_r_k, rwkv_ln_g, rwkv_ln_b, rwkv_w_o, conv_w_pw1, conv_b_pw1, conv_w_dw, conv_b_dw, conv_ln_g, conv_ln_b, conv_w_pw2, conv_b_pw2, moe_w_group, moe_b_group, moe_w_expert, moe_b_expert, moe_w_gate, moe_w_up, moe_w_down):
    B, T, D = x.shape
    N = B * T
    depth = norm1_g.shape[0]
    assert depth == 2 and D % QUAD == 0 and T % CHUNK == 0
    row = lambda v: v.reshape(1, -1)
    bf = lambda w: w.astype(BF16)

    r, k, v, wl, a, g = _rwkv_pre(
        x, row(norm1_g[0]), rwkv_mu[0], bf(rwkv_w_rkv[0]), row(rwkv_w0[0]),
        bf(_pad_cols(rwkv_w1[0], LANES)), bf(_pad_rows(rwkv_w2[0], LANES)), row(rwkv_a0[0]),
        bf(_pad_cols(rwkv_a1[0], LANES)), bf(_pad_rows(rwkv_a2[0], LANES)),
        bf(rwkv_g1[0]), bf(rwkv_g2[0]), tt=_tile(T, 512))
    y = _recurrence(r, k, v, wl, a, row(rwkv_k_k[0]), row(rwkv_k_a[0]), rwkv_r_k[0].reshape(1, -1),
                    row(rwkv_ln_g[0]), row(rwkv_ln_b[0]))
    wr_hi, wr_lo, br = _router_weights(moe_w_group[0], moe_b_group[0], moe_w_expert[0], moe_b_expert[0])
    h1, xg, route, counts = _rwkv_post(y.reshape(N, D), g.reshape(N, D), x.reshape(N, D), bf(rwkv_w_o[0]),
                                       row(norm2_g[0]), wr_hi, wr_lo, br, tt=_tile(N, 512))
    moe_tiles = dict(tm=_tile(N, 1024), tt=_tile(N, 1024), eb=EXPERTS_PER_GROUP)
    moe_w = (bf(moe_w_gate), bf(moe_w_up), bf(moe_w_down))
    h = _moe(route, counts, xg, h1, *moe_w, row(final_g),
             layer=0, final_norm=False, **moe_tiles)

    wr_hi, wr_lo, br = _router_weights(moe_w_group[1], moe_b_group[1], moe_w_expert[1], moe_b_expert[1])
    h1, xg, route, counts = _conformer(
        h.reshape(B, T, D), row(norm1_g[1]), bf(conv_w_pw1[0]), row(conv_b_pw1[0]), conv_w_dw[0], row(conv_b_dw[0]),
        row(conv_ln_g[0]), row(conv_ln_b[0]), bf(conv_w_pw2[0]), row(conv_b_pw2[0]),
        row(norm2_g[1]), wr_hi, wr_lo, br, tt=_tile(T, 256))
    out = _moe(route, counts, xg, h1, *moe_w, row(final_g),
               layer=1, final_norm=True, **moe_tiles)
    return out.reshape(B, T, D)
```

```python
import functools
import math

import jax
import jax.numpy as jnp
from jax import lax
from jax.experimental import pallas as pl
from jax.experimental.pallas import tpu as pltpu

F32 = jnp.float32
BF16 = jnp.bfloat16

HEAD = 64
LNX_EPS = HEAD * 1e-5
CONV_WIDTH = 31
CONV_LN_EPS = 1e-5
N_GROUPS = 4
EXPERTS_PER_GROUP = 8
N_EXPERTS = N_GROUPS * EXPERTS_PER_GROUP
RMS_EPS = 1e-6

LANES = 128
SUBLANES = 8
QUAD = 4 * HEAD
CHUNK = 64
VMEM_LIMIT = 56 * 1024 * 1024

NEG_BIG = -1e30


def _dot(a, b):
    return jnp.dot(a, b, preferred_element_type=F32)


def _dot_nt(a, b):
    return lax.dot_general(a, b, (((1,), (1,)), ((), ())), preferred_element_type=F32)


def _dot_tn(a, b):
    return lax.dot_general(a, b, (((0,), (0,)), ((), ())), preferred_element_type=F32)


def _split2(x):
    hi = x.astype(BF16)
    lo = (x - hi.astype(F32)).astype(BF16)
    return hi, lo


def _sigmoid(x):
    return 1.0 / (1.0 + jnp.exp(-x))


def _rms(x, g):
    return x * lax.rsqrt(jnp.mean(x * x, axis=-1, keepdims=True) + RMS_EPS) * g


def _rwkv_pre_kernel(x_ref, n1g_ref, mu_ref, wrkv_ref, w0_ref, w1_ref, w2_ref, a0_ref, a1_ref, a2_ref,
                     g1_ref, g2_ref, r_ref, k_ref, v_ref, wl_ref, a_ref, g_ref, prev_ref):
    i = pl.program_id(1)
    tt = x_ref.shape[0]
    hn = _rms(x_ref[...], n1g_ref[...])

    @pl.when(i == 0)
    def _():
        prev_ref[...] = jnp.zeros_like(prev_ref)

    prev = prev_ref[0:1, :]
    rolled = pltpu.roll(hn, 1, 0)
    row = lax.broadcasted_iota(jnp.int32, hn.shape, 0)
    shifted = jnp.where(row == 0, prev, rolled)
    prev_ref[0:1, :] = hn[tt - 1:tt, :]
    xx = shifted - hn

    def mix(j):
        return (hn + xx * mu_ref[j:j + 1, :]).astype(BF16)

    r_ref[...] = _dot(mix(0), wrkv_ref[0]).astype(BF16)
    k_ref[...] = _dot(mix(2), wrkv_ref[1]).astype(BF16)
    v_ref[...] = _dot(mix(3), wrkv_ref[2]).astype(BF16)
    zw = w0_ref[...] + _dot(jnp.tanh(_dot(mix(1), w1_ref[...])).astype(BF16), w2_ref[...])
    wl_ref[...] = -math.exp(-0.5) * _sigmoid(zw)
    a = _sigmoid(a0_ref[...] + _dot(_dot(mix(4), a1_ref[...]).astype(BF16), a2_ref[...]))
    a_ref[...] = a.astype(BF16)
    g_ref[...] = _dot(_sigmoid(_dot(mix(5), g1_ref[...])).astype(BF16), g2_ref[...]).astype(BF16)


def _rwkv_pre(x, n1g, mu, wrkv, w0, w1, w2, a0, a1, a2, g1, g2, *, tt):
    B, T, D = x.shape
    tok = pl.BlockSpec((None, tt, D), lambda b, i: (b, i, 0))

    def full(arr):
        nd = arr.ndim
        return pl.BlockSpec(arr.shape, lambda b, i: (0,) * nd)

    params = (n1g, mu, wrkv, w0, w1, w2, a0, a1, a2, g1, g2)
    out = lambda dt: jax.ShapeDtypeStruct((B, T, D), dt)
    return pl.pallas_call(
        _rwkv_pre_kernel,
        out_shape=(out(BF16), out(BF16), out(BF16), out(F32), out(BF16), out(BF16)),
        grid=(B, T // tt),
        in_specs=[tok] + [full(p) for p in params],
        out_specs=(tok,) * 6,
        scratch_shapes=[pltpu.VMEM((8, D), F32)],
        compiler_params=pltpu.CompilerParams(
            dimension_semantics=("arbitrary", "arbitrary"), vmem_limit_bytes=VMEM_LIMIT),
        name="rwkv_pre",
    )(x, *params)


def _rec_consts(tb):
    L, Q = CHUNK, QUAD
    ri = lax.broadcasted_iota(jnp.int32, (Q, Q), 0)
    ci = lax.broadcasted_iota(jnp.int32, (Q, Q), 1)
    bd = (jnp.right_shift(ri, 6) == jnp.right_shift(ci, 6)).astype(F32)
    ei = lax.broadcasted_iota(jnp.int32, (L, 4 * L), 0)
    ej = lax.broadcasted_iota(jnp.int32, (L, 4 * L), 1)
    eye_lp = (jnp.bitwise_and(ej, L - 1) == ei).astype(F32)
    mi = lax.broadcasted_iota(jnp.int32, (2 * L, 8 * L), 0)
    mj = lax.broadcasted_iota(jnp.int32, (2 * L, 8 * L), 1)
    t_row = jnp.bitwise_and(mi, L - 1)
    t_col = jnp.bitwise_and(mj, L - 1)
    score_mask = ((t_col < t_row) | ((mi >= L) & (t_col == t_row))).astype(F32)
    bi = lax.broadcasted_iota(jnp.int32, (tb, tb), 0)
    bj = lax.broadcasted_iota(jnp.int32, (tb, tb), 1)
    tri_bd = ((jnp.right_shift(bi, 6) == jnp.right_shift(bj, 6)) & (bj <= bi)).astype(BF16)
    return bd, eye_lp, score_mask, tri_bd


def _recurrence_kernel(r_ref, k_ref, v_ref, wl_ref, a_ref, kk_ref, ka_ref, rk_ref, lg_ref, lb_ref,
                       y_ref, z_ref, rt_s, at_s, kt_s, bt_s, kh_s, bh_s, vb_s, w_s, arb_s, u0_s, y0_s, bon_s,
                       kbt_s, gcol_s):
    assert CHUNK == HEAD and QUAD == 4 * CHUNK
    L, Q = CHUNK, QUAD
    t = pl.program_id(0)
    nc, _, width = r_ref.shape
    tb = nc * L
    nq = width // Q

    def load(ref, cols):
        return jnp.concatenate([ref[b, :, cols].astype(F32) for b in range(nc)], axis=0)
    bd_f, eye_lp, score_mask_f, tri_bd = _rec_consts(tb)
    bd_b = bd_f.astype(BF16)

    def bdsum(x, passes=1):
        if passes == 1:
            return _dot(x.astype(BF16), bd_b)
        hi, lo = _split2(x)
        return _dot(hi, bd_b) + _dot(lo, bd_b)

    def stack(xb):
        return jnp.concatenate([xb] * 4, axis=0) * bd_b

    @pl.when(t == 0)
    def _():
        z_ref[...] = jnp.zeros_like(z_ref)

    for q in range(nq):
        cols = slice(q * Q, (q + 1) * Q)
        r, k, v, wl, a = (load(ref, cols) for ref in (r_ref, k_ref, v_ref, wl_ref, a_ref))
        kk = k * kk_ref[:, cols]
        kk = kk * lax.rsqrt(jnp.maximum(bdsum(kk * kk, passes=2), 1e-24))
        k2 = k * (1.0 + (a - 1.0) * ka_ref[:, cols])
        bb = kk * a
        bon_s[:, cols] = bdsum(r * k2 * rk_ref[:, cols]) * v
        w_hi, w_lo = _split2(wl)
        cw = _dot(tri_bd, w_hi) + _dot(tri_bd, w_lo)
        rt_s[:, cols] = (r * jnp.exp(cw)).astype(BF16)
        at_s[:, cols] = (-kk * jnp.exp(cw - wl)).astype(BF16)
        g_inv = jnp.exp(-cw)
        kt_s[:, cols] = (k2 * g_inv).astype(BF16)
        bt_s[:, cols] = (bb * g_inv).astype(BF16)
        vb_s[:, cols] = v.astype(BF16)
        for c in range(nc):
            rows = slice(c * L, (c + 1) * L)
            tot = cw[(c + 1) * L - 1:(c + 1) * L, :]
            g_rem = jnp.exp(tot - cw[rows, :])
            kh_s[rows, cols] = (k2[rows, :] * g_rem).astype(BF16)
            bh_s[rows, cols] = (bb[rows, :] * g_rem).astype(BF16)
            gcol_s[c * nq + q] = jnp.broadcast_to(jnp.transpose(jnp.exp(tot)), (Q, LANES))

    tiles = [(slice(c * L, (c + 1) * L), slice(q * Q, (q + 1) * Q), c * nq + q)
             for c in range(nc) for q in range(nq)]
    score_mask = score_mask_f > 0.5
    for rows, cols, j in tiles:
        rhs = jnp.concatenate([stack(kt_s[rows, cols]), stack(bt_s[rows, cols])], axis=0)
        lhs = jnp.concatenate([at_s[rows, cols], rt_s[rows, cols]], axis=0)
        sc = jnp.where(score_mask, _dot_nt(lhs, rhs), 0.0)
        kt_s[rows, cols] = sc[:L, :4 * L].astype(BF16)
        bt_s[rows, cols] = sc[L:, :4 * L].astype(BF16)
        arb_s[rows, cols] = sc[L:, 4 * L:].astype(BF16)
        a_ab = sc[:L, 4 * L:]
        w_s[rows, cols] = a_ab.astype(BF16)
        u0_s[rows, cols] = eye_lp + a_ab
        khbh = jnp.concatenate([kh_s[rows, cols], bh_s[rows, cols]], axis=0).astype(F32)
        kbt_s[j] = jnp.transpose(khbh).astype(BF16)
    for rows, cols, j in tiles:
        nb = w_s[rows, cols]
        w_s[rows, cols] = _dot(nb, stack(nb)).astype(BF16)
    for _ in range(int(math.log2(L)) - 2):
        for rows, cols, j in tiles:
            mb = w_s[rows, cols]
            tinv = u0_s[rows, cols]
            res = _dot(jnp.concatenate([mb, tinv.astype(BF16)], axis=0), stack(mb))
            w_s[rows, cols] = res[:L].astype(BF16)
            u0_s[rows, cols] = tinv + res[L:]
    for rows, cols, j in tiles:
        lhs = jnp.concatenate([kt_s[rows, cols], bt_s[rows, cols]], axis=0)
        pu_y0 = _dot(lhs, stack(vb_s[rows, cols]))
        kt_s[rows, cols] = pu_y0[:L].astype(BF16)
        y0_s[rows, cols] = pu_y0[L:]
    for rows, cols, j in tiles:
        tinv = u0_s[rows, cols]
        tinv = tinv + _dot(tinv.astype(BF16), stack(w_s[rows, cols]))
        wu = _dot(tinv.astype(BF16),
                  jnp.concatenate([stack(at_s[rows, cols]), stack(kt_s[rows, cols])], axis=1))
        w_s[rows, cols] = wu[:, :Q].astype(BF16)
        u0_s[rows, cols] = wu[:, Q:]

    for rows, cols, j in tiles:
        z = z_ref[j]
        wr = _dot(jnp.concatenate([w_s[rows, cols], rt_s[rows, cols]], axis=0), z.astype(BF16))
        usb = (wr[:L] + u0_s[rows, cols]).astype(BF16)
        y0_s[rows, cols] = y0_s[rows, cols] + wr[L:] + _dot(arb_s[rows, cols], stack(usb))
        upd = _dot(kbt_s[j], jnp.concatenate([vb_s[rows, cols], usb], axis=0))
        g = gcol_s[j]
        z_ref[j] = z * jnp.concatenate([g, g], axis=1) + upd * bd_f

    for q in range(nq):
        cols = slice(q * Q, (q + 1) * Q)
        y = y0_s[:, cols]
        yc = y - bdsum(y) * (1.0 / HEAD)
        var = bdsum(yc * yc) * (1.0 / HEAD)
        out = bon_s[:, cols] + yc * lax.rsqrt(var + LNX_EPS) * lg_ref[:, cols] + lb_ref[:, cols]
        for b in range(nc):
            y_ref[b, :, cols] = out[b * L:(b + 1) * L, :].astype(y_ref.dtype)


def _recurrence(r, k, v, wl, a, k_k, k_a, r_k, ln_g, ln_b):
    B, T, D = r.shape
    tok = pl.BlockSpec((B, CHUNK, D), lambda t: (0, t, 0))
    par = pl.BlockSpec((1, D), lambda t: (0, 0))
    n_tiles = B * (D // QUAD)
    blk_bf = pltpu.VMEM((B * CHUNK, D), BF16)
    blk_f32 = pltpu.VMEM((B * CHUNK, D), F32)
    return pl.pallas_call(
        _recurrence_kernel,
        out_shape=jax.ShapeDtypeStruct((B, T, D), BF16),
        grid=(T // CHUNK,),
        in_specs=[tok] * 5 + [par] * 5,
        out_specs=tok,
        scratch_shapes=[pltpu.VMEM((n_tiles, QUAD, QUAD), F32)] + [blk_bf] * 9 + [blk_f32] * 3
        + [pltpu.VMEM((n_tiles, QUAD, 2 * CHUNK), BF16), pltpu.VMEM((n_tiles, QUAD, LANES), F32)],
        compiler_params=pltpu.CompilerParams(
            dimension_semantics=("arbitrary",), vmem_limit_bytes=VMEM_LIMIT),
        name="rwkv_recurrence",
    )(r, k, v, wl, a, k_k, k_a, r_k, ln_g, ln_b)


def _route(h1, n2g, wr_hi, wr_lo, br, xg_ref, route_ref, counts_ref, cnt_ref, is_first):
    d = h1.shape[1]
    hn = _rms(h1, n2g)
    x_hi, x_lo = _split2(hn)
    logits = _dot(x_hi, wr_hi) + _dot(x_lo, wr_hi) + _dot(x_hi, wr_lo) + br
    lane = lax.broadcasted_iota(jnp.int32, logits.shape, 1)

    def first_argmax(vals):
        mx = jnp.max(vals, axis=-1, keepdims=True)
        idx = jnp.min(jnp.where(vals == mx, lane, 4 * LANES), axis=-1, keepdims=True)
        return mx, idx

    lg = jnp.where((lane >= N_EXPERTS) & (lane < N_EXPERTS + N_GROUPS), logits, NEG_BIG)
    g_max, g_lane = first_argmax(lg)
    p_top = 1.0 / jnp.sum(jnp.exp(lg - g_max), axis=-1, keepdims=True)
    g_idx = g_lane - N_EXPERTS
    in_group = (lane >= g_idx * EXPERTS_PER_GROUP) & (lane < (g_idx + 1) * EXPERTS_PER_GROUP)
    le = jnp.where(in_group, logits, NEG_BIG)
    m1, i1 = first_argmax(le)
    m2, i2 = first_argmax(jnp.where(lane == i1, NEG_BIG, le))
    e2 = jnp.exp(m2 - m1)
    w1 = p_top / (1.0 + e2)
    w2 = p_top * e2 / (1.0 + e2)
    gates = jnp.where(lane == i1, w1, jnp.where(lane == i2, w2, 0.0))
    xg_ref[:, :d] = hn
    xg_ref[:, d:] = gates

    @pl.when(is_first)
    def _():
        cnt_ref[...] = jnp.zeros_like(cnt_ref)

    rows = h1.shape[0]
    onehot = lane == g_idx
    ri = lax.broadcasted_iota(jnp.int32, (rows, rows), 0)
    ci = lax.broadcasted_iota(jnp.int32, (rows, rows), 1)
    before = _dot((ci < ri).astype(F32).astype(BF16), onehot.astype(F32).astype(BF16)) + cnt_ref[0:1, :]
    rank = jnp.sum(jnp.where(onehot, before, 0.0), axis=-1, keepdims=True)
    route_ref[...] = jnp.where(lane == 0, g_idx.astype(F32), jnp.where(lane == 1, rank, 0.0))
    total = cnt_ref[0:1, :] + jnp.sum(onehot.astype(F32), axis=0, keepdims=True)
    cnt_ref[0:1, :] = total
    counts_ref[...] = total


def _rwkv_post_kernel(y_ref, g_ref, x_ref, wo_ref, n2g_ref, wrh_ref, wrl_ref, br_ref,
                      h1_ref, xg_ref, route_ref, counts_ref, cnt_ref):
    yg = (y_ref[...] * g_ref[...]).astype(BF16)
    h1 = x_ref[...] + _dot(yg, wo_ref[...])
    h1_ref[...] = h1
    _route(h1, n2g_ref[...], wrh_ref[...], wrl_ref[...], br_ref[...], xg_ref, route_ref, counts_ref, cnt_ref,
           pl.program_id(0) == 0)


def _tail_outputs(N, D, tt, tok_map):
    shapes = (jax.ShapeDtypeStruct((N, D), F32), jax.ShapeDtypeStruct((N, D + LANES), F32),
              jax.ShapeDtypeStruct((N, LANES), F32), jax.ShapeDtypeStruct((1, LANES), F32))
    specs = (pl.BlockSpec((tt, D), tok_map), pl.BlockSpec((tt, D + LANES), tok_map),
             pl.BlockSpec((tt, LANES), tok_map), pl.BlockSpec((1, LANES), lambda *_: (0, 0)))
    return shapes, specs


def _rwkv_post(y, g, x, wo, n2g, wr_hi, wr_lo, br, *, tt):
    N, D = x.shape
    tok_map = lambda i: (i, 0)
    tok = pl.BlockSpec((tt, D), tok_map)

    def full(arr):
        nd = arr.ndim
        return pl.BlockSpec(arr.shape, lambda i: (0,) * nd)

    params = (wo, n2g, wr_hi, wr_lo, br)
    out_shape, out_specs = _tail_outputs(N, D, tt, tok_map)
    return pl.pallas_call(
        _rwkv_post_kernel,
        out_shape=out_shape,
        grid=(N // tt,),
        in_specs=[tok, tok, tok] + [full(p) for p in params],
        out_specs=out_specs,
        scratch_shapes=[pltpu.VMEM((8, LANES), F32)],
        compiler_params=pltpu.CompilerParams(
            dimension_semantics=("arbitrary",), vmem_limit_bytes=VMEM_LIMIT),
        name="rwkv_post",
    )(y, g, x, *params)


HALO = 32


def _conformer_kernel(x_ref, n1g_ref, wpw1_ref, bpw1_ref, wdw_ref, bdw_ref, lng_ref, lnb_ref, wpw2_ref, bpw2_ref,
                      n2g_ref, wrh_ref, wrl_ref, br_ref, h1_ref, xg_ref, route_ref, counts_ref, ubuf_ref, cnt_ref):
    i = pl.program_id(1)
    tt, D = x_ref.shape
    x = x_ref[...]
    hn = _rms(x, n1g_ref[...]).astype(BF16)
    u = _dot(hn, wpw1_ref[...]) + bpw1_ref[...]
    u = u[:, :D] * _sigmoid(u[:, D:])

    @pl.when(i == 0)
    def _():
        ubuf_ref[0:HALO, :] = jnp.zeros((HALO, D), F32)
        ubuf_ref[HALO + tt:, :] = jnp.zeros((SUBLANES, D), F32)

    ubuf_ref[HALO:HALO + tt, :] = u
    base = HALO - (CONV_WIDTH - 1)
    acc = jnp.zeros((tt, D), F32) + bdw_ref[...]
    for s in range(SUBLANES):
        part = None
        for q in range((base + CONV_WIDTH - 1) // SUBLANES + 1):
            j = q * SUBLANES + s - base
            if 0 <= j < CONV_WIDTH:
                term = wdw_ref[j:j + 1, :] * ubuf_ref[q * SUBLANES:q * SUBLANES + tt + SUBLANES, :]
                part = term if part is None else part + term
        acc = acc + part[s:s + tt, :]
    ubuf_ref[0:HALO, :] = ubuf_ref[tt:tt + HALO, :]

    m = jnp.mean(acc, axis=-1, keepdims=True)
    c = acc - m
    var = jnp.mean(c * c, axis=-1, keepdims=True)
    c = c * lax.rsqrt(var + CONV_LN_EPS) * lng_ref[...] + lnb_ref[...]
    c = c * _sigmoid(c)
    h1 = x + _dot(c.astype(BF16), wpw2_ref[...]) + bpw2_ref[...]
    h1_ref[...] = h1
    _route(h1, n2g_ref[...], wrh_ref[...], wrl_ref[...], br_ref[...], xg_ref, route_ref, counts_ref, cnt_ref,
           (pl.program_id(0) == 0) & (i == 0))


def _conformer(x, n1g, wpw1, bpw1, wdw, bdw, lng, lnb, wpw2, bpw2, n2g, wr_hi, wr_lo, br, *, tt):
    B, T, D = x.shape
    nt = T // tt
    tok = pl.BlockSpec((None, tt, D), lambda b, i: (b, i, 0))

    def full(arr):
        nd = arr.ndim
        return pl.BlockSpec(arr.shape, lambda b, i: (0,) * nd)

    params = (n1g, wpw1, bpw1, wdw, bdw, lng, lnb, wpw2, bpw2, n2g, wr_hi, wr_lo, br)
    out_shape, out_specs = _tail_outputs(B * T, D, tt, lambda b, i: (b * nt + i, 0))
    return pl.pallas_call(
        _conformer_kernel,
        out_shape=out_shape,
        grid=(B, nt),
        in_specs=[tok] + [full(p) for p in params],
        out_specs=out_specs,
        scratch_shapes=[pltpu.VMEM((HALO + tt + SUBLANES, D), F32), pltpu.VMEM((8, LANES), F32)],
        compiler_params=pltpu.CompilerParams(
            dimension_semantics=("arbitrary", "arbitrary"), vmem_limit_bytes=VMEM_LIMIT),
        name="conformer",
    )(x, *params)


def _dispatch_kernel(pos_ref, pad_ref, xg_ref, xs_ref, zero_ref, sem, zsem):
    tt = xg_ref.shape[0]
    i = pl.program_id(0)
    base = i * tt
    n_seg = 2 * N_GROUPS

    def zero_fill(action):
        for s in range(n_seg):
            n = pl.multiple_of(pad_ref[n_seg + s], SUBLANES)
            start = pl.multiple_of(pad_ref[s], SUBLANES)
            copy = pltpu.make_async_copy(zero_ref.at[pl.ds(0, n)], xs_ref.at[pl.ds(start, n)], zsem)
            pl.when(n > 0)(getattr(copy, action))
        for g in range(N_GROUPS):
            start, n = pad_ref[2 * n_seg + g], pad_ref[2 * n_seg + N_GROUPS + g]
            for u in range(SUBLANES - 1):
                copy = pltpu.make_async_copy(zero_ref.at[pl.ds(0, 1)], xs_ref.at[pl.ds(start + u, 1)], zsem)
                pl.when(u < n)(getattr(copy, action))

    @pl.when(i == 0)
    def _():
        zero_ref[...] = jnp.zeros_like(zero_ref)
        zero_fill("start")

    def issue(jj, carry):
        j0 = pl.multiple_of(jj * SUBLANES, SUBLANES)
        for u in range(SUBLANES):
            src = xg_ref.at[pl.ds(j0, SUBLANES)].at[pl.ds(u, 1)]
            pltpu.make_async_copy(src, xs_ref.at[pl.ds(pos_ref[base + j0 + u], 1)], sem).start()
        return carry

    lax.fori_loop(0, tt // SUBLANES, issue, 0)
    pltpu.make_async_copy(xg_ref, xs_ref.at[pl.ds(0, tt)], sem).wait()

    @pl.when(i == 0)
    def _():
        zero_fill("wait")


def _dispatch(pos, pad, xg, n_rows, *, tt, tm):
    N, W = xg.shape
    return pl.pallas_call(
        _dispatch_kernel,
        out_shape=jax.ShapeDtypeStruct((n_rows, W), F32),
        grid_spec=pltpu.PrefetchScalarGridSpec(
            num_scalar_prefetch=2,
            grid=(N // tt,),
            in_specs=[pl.BlockSpec((tt, W), lambda i, pos, pad: (i, 0))],
            out_specs=pl.BlockSpec(memory_space=pl.ANY),
            scratch_shapes=[pltpu.VMEM((tm, W), F32), pltpu.SemaphoreType.DMA, pltpu.SemaphoreType.DMA]),
        compiler_params=pltpu.CompilerParams(
            dimension_semantics=("arbitrary",), vmem_limit_bytes=VMEM_LIMIT, disable_bounds_checks=True),
        name="moe_dispatch",
    )(pos, pad, xg)


def _moe_group_kernel(tg_ref, nu_ref, xs_ref, wg_ref, wu_ref, wd_ref, ys_ref, *, eb):
    i = pl.program_id(0)
    e = pl.program_id(1)
    d = xs_ref.shape[1] - LANES

    @pl.when(i < nu_ref[0])
    def _():
        x = xs_ref[:, :d].astype(BF16)
        gates = xs_ref[:, d:]
        lane = lax.broadcasted_iota(jnp.int32, gates.shape, 1)
        first = tg_ref[i] * EXPERTS_PER_GROUP + e * eb
        zs = []
        for q in range(eb):
            ge = jnp.sum(jnp.where(lane == first + q, gates, 0.0), axis=-1, keepdims=True)
            zg = _dot(x, wg_ref[q])
            zu = _dot(x, wu_ref[q])
            zs.append((zg * _sigmoid(zg) * zu * ge).astype(BF16))
        contrib = _dot(jnp.concatenate(zs, axis=-1), wd_ref[...].reshape(-1, d))

        contrib = contrib.reshape(ys_ref.shape)

        @pl.when(e == 0)
        def _():
            ys_ref[...] = contrib

        @pl.when(e != 0)
        def _():
            ys_ref[...] += contrib

    @pl.when((i >= nu_ref[0]) & (e == 0))
    def _():
        ys_ref[...] = jnp.zeros_like(ys_ref)


def _moe_group(tile_group, n_used, xs, wg, wu, wd, *, layer, tm, eb):
    n_rows, W = xs.shape
    _, E, D, F = wg.shape
    steps = EXPERTS_PER_GROUP // eb
    row_map = lambda i, e, tg, nu: (jnp.minimum(i, nu[0] - 1), 0)
    w_map = lambda i, e, tg, nu: (layer, tg[i] * steps + e, 0, 0)
    return pl.pallas_call(
        functools.partial(_moe_group_kernel, eb=eb),
        out_shape=jax.ShapeDtypeStruct((n_rows, D // LANES, LANES), F32),
        grid_spec=pltpu.PrefetchScalarGridSpec(
            num_scalar_prefetch=2,
            grid=(n_rows // tm, steps),
            in_specs=[pl.BlockSpec((tm, W), row_map),
                      pl.BlockSpec((None, eb, D, F), w_map), pl.BlockSpec((None, eb, D, F), w_map),
                      pl.BlockSpec((None, eb, F, D), w_map)],
            out_specs=pl.BlockSpec((tm, D // LANES, LANES), lambda i, e, tg, nu: (i, 0, 0))),
        compiler_params=pltpu.CompilerParams(
            dimension_semantics=("arbitrary", "arbitrary"), vmem_limit_bytes=VMEM_LIMIT),
        name="moe_group",
    )(tile_group, n_used, xs, wg, wu, wd)


def _combine_kernel(pos_ref, h1_ref, fg_ref, ys_ref, out_ref, buf_ref, sem, *, final_norm):
    tt = h1_ref.shape[0]
    i = pl.program_id(0)
    slot = lax.rem(i, 2)

    def gather(tile, s):
        base = tile * tt

        def issue(jj, carry):
            j0 = jj * SUBLANES
            for u in range(SUBLANES):
                dst = buf_ref.at[s].at[pl.ds(j0 + u, 1)]
                pltpu.make_async_copy(ys_ref.at[pl.ds(pos_ref[base + j0 + u], 1)], dst, sem.at[s]).start()
            return carry

        lax.fori_loop(0, tt // SUBLANES, issue, 0)

    @pl.when(i == 0)
    def _():
        gather(0, 0)

    @pl.when(i + 1 < pl.num_programs(0))
    def _():
        gather(i + 1, 1 - slot)

    pltpu.make_async_copy(ys_ref.at[pl.ds(0, tt)], buf_ref.at[slot], sem.at[slot]).wait()
    h = h1_ref[...] + buf_ref[slot].reshape(h1_ref.shape)
    out_ref[...] = _rms(h, fg_ref[...]) if final_norm else h


def _combine(pos, h1, fg, ys, *, tt, final_norm):
    N, D = h1.shape
    tok = pl.BlockSpec((tt, D), lambda i, pos: (i, 0))
    return pl.pallas_call(
        functools.partial(_combine_kernel, final_norm=final_norm),
        out_shape=jax.ShapeDtypeStruct((N, D), F32),
        grid_spec=pltpu.PrefetchScalarGridSpec(
            num_scalar_prefetch=1,
            grid=(N // tt,),
            in_specs=[tok, pl.BlockSpec((1, D), lambda i, pos: (0, 0)), pl.BlockSpec(memory_space=pl.ANY)],
            out_specs=tok,
            scratch_shapes=[pltpu.VMEM((2, tt, D // LANES, LANES), F32), pltpu.SemaphoreType.DMA((2,))]),
        compiler_params=pltpu.CompilerParams(
            dimension_semantics=("arbitrary",), vmem_limit_bytes=VMEM_LIMIT, disable_bounds_checks=True),
        name="moe_combine_final" if final_norm else "moe_combine",
    )(pos, h1, fg, ys)


def _moe(route, counts, xg, h1, wg, wu, wd, fg, *, layer, tm, tt, eb, final_norm):
    N, D = h1.shape
    group = route[:, 0].astype(jnp.int32)
    rank = route[:, 1].astype(jnp.int32)
    cnt = counts[0, :N_GROUPS].astype(jnp.int32)
    padded = (cnt + tm - 1) // tm * tm
    ends = jnp.cumsum(padded)
    pos = rank
    for g in range(N_GROUPS):
        pos = pos + jnp.where(group == g, ends[g] - padded[g], 0)
    n_tiles = N // tm + N_GROUPS
    n_used = ends[-1:] // tm
    tile_group = jnp.sum((jnp.arange(n_tiles)[:, None] * tm >= ends[None, :]).astype(jnp.int32), axis=1)
    tile_group = jnp.minimum(tile_group, tile_group[n_used[0] - 1])
    tail = ends[-1] + jnp.arange(N_GROUPS) * tm
    pad_start, pad_rows = ends - padded + cnt, padded - cnt
    head_rows = jnp.minimum(pad_rows, (-pad_start) % SUBLANES)
    pad = jnp.concatenate([pad_start + head_rows, tail, pad_rows - head_rows, jnp.where(tail < n_tiles * tm, tm, 0),
                           pad_start, head_rows]).astype(jnp.int32)
    xs = _dispatch(pos, pad, xg, n_tiles * tm, tt=tt, tm=tm)
    ys = _moe_group(tile_group, n_used, xs, wg, wu, wd, layer=layer, tm=tm, eb=eb)
    return _combine(pos, h1, fg, ys, tt=_tile(N, 512), final_norm=final_norm)


def _router_weights(w_group, b_group, w_expert, b_expert):
    D = w_group.shape[0]
    w = jnp.zeros((D, LANES), F32).at[:, :N_EXPERTS].set(w_expert).at[:, N_EXPERTS:N_EXPERTS + N_GROUPS].set(w_group)
    b = jnp.zeros((1, LANES), F32).at[0, :N_EXPERTS].set(b_expert).at[0, N_EXPERTS:N_EXPERTS + N_GROUPS].set(b_group)
    hi = w.astype(BF16)
    lo = (w - hi.astype(F32)).astype(BF16)
    return hi, lo, b


def _pad_cols(w, n):
    return jnp.pad(w, ((0, 0), (0, n - w.shape[1])))


def _pad_rows(w, n):
    return jnp.pad(w, ((0, n - w.shape[0]), (0, 0)))


def _tile(n, want):
    t = min(n, want)
    while n % t:
        t //= 2
    return t


def kernel(x, norm1_g, norm2_g, final_g, rwkv_mu, rwkv_w_rkv, rwkv_w0, rwkv_w1, rwkv_w2, rwkv_a0, rwkv_a1, rwkv_a2, rwkv_g1, rwkv_g2, rwkv_k_k, rwkv_k_a, rwkv_r_k, rwkv_ln_g, rwkv_ln_b, rwkv_w_o, conv_w_pw1, conv_b_pw1, conv_w_dw, conv_b_dw, conv_ln_g, conv_ln_b, conv_w_pw2, conv_b_pw2, moe_w_group, moe_b_group, moe_w_expert, moe_b_expert, moe_w_gate, moe_w_up, moe_w_down):
    B, T, D = x.shape
    N = B * T
    depth = norm1_g.shape[0]
    assert depth == 2 and D % QUAD == 0 and T % CHUNK == 0
    row = lambda v: v.reshape(1, -1)
    bf = lambda w: w.astype(BF16)

    r, k, v, wl, a, g = _rwkv_pre(
        x, row(norm1_g[0]), rwkv_mu[0], bf(rwkv_w_rkv[0]), row(rwkv_w0[0]),
        bf(_pad_cols(rwkv_w1[0], LANES)), bf(_pad_rows(rwkv_w2[0], LANES)), row(rwkv_a0[0]),
        bf(_pad_cols(rwkv_a1[0], LANES)), bf(_pad_rows(rwkv_a2[0], LANES)),
        bf(rwkv_g1[0]), bf(rwkv_g2[0]), tt=_tile(T, 512))
    y = _recurrence(r, k, v, wl, a, row(rwkv_k_k[0]), row(rwkv_k_a[0]), rwkv_r_k[0].reshape(1, -1),
                    row(rwkv_ln_g[0]), row(rwkv_ln_b[0]))
    wr_hi, wr_lo, br = _router_weights(moe_w_group[0], moe_b_group[0], moe_w_expert[0], moe_b_expert[0])
    h1, xg, route, counts = _rwkv_post(y.reshape(N, D), g.reshape(N, D), x.reshape(N, D), bf(rwkv_w_o[0]),
                                       row(norm2_g[0]), wr_hi, wr_lo, br, tt=_tile(N, 512))
    moe_tiles = dict(tm=_tile(N, 1024), tt=_tile(N, 1024), eb=EXPERTS_PER_GROUP)
    moe_w = (bf(moe_w_gate), bf(moe_w_up), bf(moe_w_down))
    h = _moe(route, counts, xg, h1, *moe_w, row(final_g),
             layer=0, final_norm=False, **moe_tiles)

    wr_hi, wr_lo, br = _router_weights(moe_w_group[1], moe_b_group[1], moe_w_expert[1], moe_b_expert[1])
    h1, xg, route, counts = _conformer(
        h.reshape(B, T, D), row(norm1_g[1]), bf(conv_w_pw1[0]), row(conv_b_pw1[0]), conv_w_dw[0], row(conv_b_dw[0]),
        row(conv_ln_g[0]), row(conv_ln_b[0]), bf(conv_w_pw2[0]), row(conv_b_pw2[0]),
        row(norm2_g[1]), wr_hi, wr_lo, br, tt=_tile(T, 256))
    out = _moe(route, counts, xg, h1, *moe_w, row(final_g),
               layer=1, final_norm=True, **moe_tiles)
    return out.reshape(B, T, D)
```

```python
import functools
import math

import jax
import jax.numpy as jnp
from jax import lax
from jax.experimental import pallas as pl
from jax.experimental.pallas import tpu as pltpu

F32 = jnp.float32
BF16 = jnp.bfloat16

HEAD = 64
LNX_EPS = HEAD * 1e-5
CONV_WIDTH = 31
CONV_LN_EPS = 1e-5
N_GROUPS = 4
EXPERTS_PER_GROUP = 8
N_EXPERTS = N_GROUPS * EXPERTS_PER_GROUP
RMS_EPS = 1e-6

LANES = 128
SUBLANES = 8
QUAD = 4 * HEAD
CHUNK = 64
VMEM_LIMIT = 56 * 1024 * 1024

NEG_BIG = -1e30


def _dot(a, b):
    return jnp.dot(a, b, preferred_element_type=F32)


def _dot_nt(a, b):
    return lax.dot_general(a, b, (((1,), (1,)), ((), ())), preferred_element_type=F32)


def _dot_tn(a, b):
    return lax.dot_general(a, b, (((0,), (0,)), ((), ())), preferred_element_type=F32)


def _split2(x):
    hi = x.astype(BF16)
    lo = (x - hi.astype(F32)).astype(BF16)
    return hi, lo


def _sigmoid(x):
    return 1.0 / (1.0 + jnp.exp(-x))


def _rms(x, g):
    return x * lax.rsqrt(jnp.mean(x * x, axis=-1, keepdims=True) + RMS_EPS) * g


def _rwkv_pre_kernel(x_ref, n1g_ref, mu_ref, wrkv_ref, w0_ref, w1_ref, w2_ref, a0_ref, a1_ref, a2_ref,
                     g1_ref, g2_ref, r_ref, k_ref, v_ref, wl_ref, a_ref, g_ref, prev_ref):
    i = pl.program_id(1)
    tt = x_ref.shape[0]
    hn = _rms(x_ref[...], n1g_ref[...])

    @pl.when(i == 0)
    def _():
        prev_ref[...] = jnp.zeros_like(prev_ref)

    prev = prev_ref[0:1, :]
    rolled = pltpu.roll(hn, 1, 0)
    row = lax.broadcasted_iota(jnp.int32, hn.shape, 0)
    shifted = jnp.where(row == 0, prev, rolled)
    prev_ref[0:1, :] = hn[tt - 1:tt, :]
    xx = shifted - hn

    def mix(j):
        return (hn + xx * mu_ref[j:j + 1, :]).astype(BF16)

    r_ref[...] = _dot(mix(0), wrkv_ref[0]).astype(BF16)
    k_ref[...] = _dot(mix(2), wrkv_ref[1]).astype(BF16)
    v_ref[...] = _dot(mix(3), wrkv_ref[2]).astype(BF16)
    zw = w0_ref[...] + _dot(jnp.tanh(_dot(mix(1), w1_ref[...])).astype(BF16), w2_ref[...])
    wl_ref[...] = -math.exp(-0.5) * _sigmoid(zw)
    a = _sigmoid(a0_ref[...] + _dot(_dot(mix(4), a1_ref[...]).astype(BF16), a2_ref[...]))
    a_ref[...] = a.astype(BF16)
    g_ref[...] = _dot(_sigmoid(_dot(mix(5), g1_ref[...])).astype(BF16), g2_ref[...]).astype(BF16)


def _rwkv_pre(x, n1g, mu, wrkv, w0, w1, w2, a0, a1, a2, g1, g2, *, tt):
    B, T, D = x.shape
    tok = pl.BlockSpec((None, tt, D), lambda b, i: (b, i, 0))

    def full(arr):
        nd = arr.ndim
        return pl.BlockSpec(arr.shape, lambda b, i: (0,) * nd)

    params = (n1g, mu, wrkv, w0, w1, w2, a0, a1, a2, g1, g2)
    out = lambda dt: jax.ShapeDtypeStruct((B, T, D), dt)
    return pl.pallas_call(
        _rwkv_pre_kernel,
        out_shape=(out(BF16), out(BF16), out(BF16), out(F32), out(BF16), out(BF16)),
        grid=(B, T // tt),
        in_specs=[tok] + [full(p) for p in params],
        out_specs=(tok,) * 6,
        scratch_shapes=[pltpu.VMEM((8, D), F32)],
        compiler_params=pltpu.CompilerParams(
            dimension_semantics=("arbitrary", "arbitrary"), vmem_limit_bytes=VMEM_LIMIT),
        name="rwkv_pre",
    )(x, *params)


def _rec_consts(tb):
    L, Q = CHUNK, QUAD
    ri = lax.broadcasted_iota(jnp.int32, (Q, Q), 0)
    ci = lax.broadcasted_iota(jnp.int32, (Q, Q), 1)
    bd = (jnp.right_shift(ri, 6) == jnp.right_shift(ci, 6)).astype(F32)
    ei = lax.broadcasted_iota(jnp.int32, (L, 4 * L), 0)
    ej = lax.broadcasted_iota(jnp.int32, (L, 4 * L), 1)
    eye_lp = (jnp.bitwise_and(ej, L - 1) == ei).astype(F32)
    mi = lax.broadcasted_iota(jnp.int32, (2 * L, 8 * L), 0)
    mj = lax.broadcasted_iota(jnp.int32, (2 * L, 8 * L), 1)
    t_row = jnp.bitwise_and(mi, L - 1)
    t_col = jnp.bitwise_and(mj, L - 1)
    score_mask = ((t_col < t_row) | ((mi >= L) & (t_col == t_row))).astype(F32)
    bi = lax.broadcasted_iota(jnp.int32, (tb, tb), 0)
    bj = lax.broadcasted_iota(jnp.int32, (tb, tb), 1)
    tri_bd = ((jnp.right_shift(bi, 6) == jnp.right_shift(bj, 6)) & (bj <= bi)).astype(BF16)
    return bd, eye_lp, score_mask, tri_bd


def _recurrence_kernel(r_ref, k_ref, v_ref, wl_ref, a_ref, kk_ref, ka_ref, rk_ref, lg_ref, lb_ref,
                       y_ref, z_ref, rt_s, at_s, kt_s, bt_s, kh_s, bh_s, vb_s, w_s, arb_s, u0_s, y0_s, bon_s,
                       kbt_s, gcol_s):
    assert CHUNK == HEAD and QUAD == 4 * CHUNK
    L, Q = CHUNK, QUAD
    t = pl.program_id(0)
    nc, _, width = r_ref.shape
    tb = nc * L
    nq = width // Q

    def load(ref, cols):
        return jnp.concatenate([ref[b, :, cols].astype(F32) for b in range(nc)], axis=0)
    bd_f, eye_lp, score_mask_f, tri_bd = _rec_consts(tb)
    bd_b = bd_f.astype(BF16)

    def bdsum(x, passes=1):
        if passes == 1:
            return _dot(x.astype(BF16), bd_b)
        hi, lo = _split2(x)
        return _dot(hi, bd_b) + _dot(lo, bd_b)

    def stack(xb):
        return jnp.concatenate([xb] * 4, axis=0) * bd_b

    @pl.when(t == 0)
    def _():
        z_ref[...] = jnp.zeros_like(z_ref)

    for q in range(nq):
        cols = slice(q * Q, (q + 1) * Q)
        r, k, v, wl, a = (load(ref, cols) for ref in (r_ref, k_ref, v_ref, wl_ref, a_ref))
        kk = k * kk_ref[:, cols]
        kk = kk * lax.rsqrt(jnp.maximum(bdsum(kk * kk, passes=2), 1e-24))
        k2 = k * (1.0 + (a - 1.0) * ka_ref[:, cols])
        bb = kk * a
        bon_s[:, cols] = bdsum(r * k2 * rk_ref[:, cols]) * v
        w_hi, w_lo = _split2(wl)
        cw = _dot(tri_bd, w_hi) + _dot(tri_bd, w_lo)
        rt_s[:, cols] = (r * jnp.exp(cw)).astype(BF16)
        at_s[:, cols] = (-kk * jnp.exp(cw - wl)).astype(BF16)
        g_inv = jnp.exp(-cw)
        kt_s[:, cols] = (k2 * g_inv).astype(BF16)
        bt_s[:, cols] = (bb * g_inv).astype(BF16)
        vb_s[:, cols] = v.astype(BF16)
        for c in range(nc):
            rows = slice(c * L, (c + 1) * L)
            tot = cw[(c + 1) * L - 1:(c + 1) * L, :]
            g_rem = jnp.exp(tot - cw[rows, :])
            kh_s[rows, cols] = (k2[rows, :] * g_rem).astype(BF16)
            bh_s[rows, cols] = (bb[rows, :] * g_rem).astype(BF16)
            gcol_s[c * nq + q] = jnp.broadcast_to(jnp.transpose(jnp.exp(tot)), (Q, LANES))

    tiles = [(slice(c * L, (c + 1) * L), slice(q * Q, (q + 1) * Q), c * nq + q)
             for c in range(nc) for q in range(nq)]
    score_mask = score_mask_f > 0.5
    for rows, cols, j in tiles:
        rhs = jnp.concatenate([stack(kt_s[rows, cols]), stack(bt_s[rows, cols])], axis=0)
        lhs = jnp.concatenate([at_s[rows, cols], rt_s[rows, cols]], axis=0)
        sc = jnp.where(score_mask, _dot_nt(lhs, rhs), 0.0)
        kt_s[rows, cols] = sc[:L, :4 * L].astype(BF16)
        bt_s[rows, cols] = sc[L:, :4 * L].astype(BF16)
        arb_s[rows, cols] = sc[L:, 4 * L:].astype(BF16)
        a_ab = sc[:L, 4 * L:]
        w_s[rows, cols] = a_ab.astype(BF16)
        u0_s[rows, cols] = eye_lp + a_ab
        khbh = jnp.concatenate([kh_s[rows, cols], bh_s[rows, cols]], axis=0).astype(F32)
        kbt_s[j] = jnp.transpose(khbh).astype(BF16)
    for rows, cols, j in tiles:
        nb = w_s[rows, cols]
        w_s[rows, cols] = _dot(nb, stack(nb)).astype(BF16)
    for _ in range(int(math.log2(L)) - 2):
        for rows, cols, j in tiles:
            mb = w_s[rows, cols]
            tinv = u0_s[rows, cols]
            res = _dot(jnp.concatenate([mb, tinv.astype(BF16)], axis=0), stack(mb))
            w_s[rows, cols] = res[:L].astype(BF16)
            u0_s[rows, cols] = tinv + res[L:]
    for rows, cols, j in tiles:
        lhs = jnp.concatenate([kt_s[rows, cols], bt_s[rows, cols]], axis=0)
        pu_y0 = _dot(lhs, stack(vb_s[rows, cols]))
        kt_s[rows, cols] = pu_y0[:L].astype(BF16)
        y0_s[rows, cols] = pu_y0[L:]
    for rows, cols, j in tiles:
        tinv = u0_s[rows, cols]
        tinv = tinv + _dot(tinv.astype(BF16), stack(w_s[rows, cols]))
        wu = _dot(tinv.astype(BF16),
                  jnp.concatenate([stack(at_s[rows, cols]), stack(kt_s[rows, cols])], axis=1))
        w_s[rows, cols] = wu[:, :Q].astype(BF16)
        u0_s[rows, cols] = wu[:, Q:]

    for rows, cols, j in tiles:
        z = z_ref[j]
        wr = _dot(jnp.concatenate([w_s[rows, cols], rt_s[rows, cols]], axis=0), z.astype(BF16))
        usb = (wr[:L] + u0_s[rows, cols]).astype(BF16)
        y0_s[rows, cols] = y0_s[rows, cols] + wr[L:] + _dot(arb_s[rows, cols], stack(usb))
        upd = _dot(kbt_s[j], jnp.concatenate([vb_s[rows, cols], usb], axis=0))
        g = gcol_s[j]
        z_ref[j] = z * jnp.concatenate([g, g], axis=1) + upd * bd_f

    for q in range(nq):
        cols = slice(q * Q, (q + 1) * Q)
        y = y0_s[:, cols]
        yc = y - bdsum(y) * (1.0 / HEAD)
        var = bdsum(yc * yc) * (1.0 / HEAD)
        out = bon_s[:, cols] + yc * lax.rsqrt(var + LNX_EPS) * lg_ref[:, cols] + lb_ref[:, cols]
        for b in range(nc):
            y_ref[b, :, cols] = out[b * L:(b + 1) * L, :].astype(y_ref.dtype)


def _recurrence(r, k, v, wl, a, k_k, k_a, r_k, ln_g, ln_b):
    B, T, D = r.shape
    tok = pl.BlockSpec((B, CHUNK, D), lambda t: (0, t, 0))
    par = pl.BlockSpec((1, D), lambda t: (0, 0))
    n_tiles = B * (D // QUAD)
    blk_bf = pltpu.VMEM((B * CHUNK, D), BF16)
    blk_f32 = pltpu.VMEM((B * CHUNK, D), F32)
    return pl.pallas_call(
        _recurrence_kernel,
        out_shape=jax.ShapeDtypeStruct((B, T, D), BF16),
        grid=(T // CHUNK,),
        in_specs=[tok] * 5 + [par] * 5,
        out_specs=tok,
        scratch_shapes=[pltpu.VMEM((n_tiles, QUAD, QUAD), F32)] + [blk_bf] * 9 + [blk_f32] * 3
        + [pltpu.VMEM((n_tiles, QUAD, 2 * CHUNK), BF16), pltpu.VMEM((n_tiles, QUAD, LANES), F32)],
        compiler_params=pltpu.CompilerParams(
            dimension_semantics=("arbitrary",), vmem_limit_bytes=VMEM_LIMIT),
        name="rwkv_recurrence",
    )(r, k, v, wl, a, k_k, k_a, r_k, ln_g, ln_b)


def _route(h1, n2g, wr_hi, wr_lo, br, xg_ref, route_ref, counts_ref, cnt_ref, is_first):
    d = h1.shape[1]
    hn = _rms(h1, n2g)
    x_hi, x_lo = _split2(hn)
    logits = _dot(x_hi, wr_hi) + _dot(x_lo, wr_hi) + _dot(x_hi, wr_lo) + br
    lane = lax.broadcasted_iota(jnp.int32, logits.shape, 1)

    def first_argmax(vals):
        mx = jnp.max(vals, axis=-1, keepdims=True)
        idx = jnp.min(jnp.where(vals == mx, lane, 4 * LANES), axis=-1, keepdims=True)
        return mx, idx

    lg = jnp.where((lane >= N_EXPERTS) & (lane < N_EXPERTS + N_GROUPS), logits, NEG_BIG)
    g_max, g_lane = first_argmax(lg)
    p_top = 1.0 / jnp.sum(jnp.exp(lg - g_max), axis=-1, keepdims=True)
    g_idx = g_lane - N_EXPERTS
    in_group = (lane >= g_idx * EXPERTS_PER_GROUP) & (lane < (g_idx + 1) * EXPERTS_PER_GROUP)
    le = jnp.where(in_group, logits, NEG_BIG)
    m1, i1 = first_argmax(le)
    m2, i2 = first_argmax(jnp.where(lane == i1, NEG_BIG, le))
    e2 = jnp.exp(m2 - m1)
    w1 = p_top / (1.0 + e2)
    w2 = p_top * e2 / (1.0 + e2)
    gates = jnp.where(lane == i1, w1, jnp.where(lane == i2, w2, 0.0))
    xg_ref[:, :d] = hn
    xg_ref[:, d:] = gates

    @pl.when(is_first)
    def _():
        cnt_ref[...] = jnp.zeros_like(cnt_ref)

    rows = h1.shape[0]
    onehot = lane == g_idx
    ri = lax.broadcasted_iota(jnp.int32, (rows, rows), 0)
    ci = lax.broadcasted_iota(jnp.int32, (rows, rows), 1)
    before = _dot((ci < ri).astype(F32).astype(BF16), onehot.astype(F32).astype(BF16)) + cnt_ref[0:1, :]
    rank = jnp.sum(jnp.where(onehot, before, 0.0), axis=-1, keepdims=True)
    route_ref[...] = jnp.where(lane == 0, g_idx.astype(F32), jnp.where(lane == 1, rank, 0.0))
    total = cnt_ref[0:1, :] + jnp.sum(onehot.astype(F32), axis=0, keepdims=True)
    cnt_ref[0:1, :] = total
    counts_ref[...] = total


def _rwkv_post_kernel(y_ref, g_ref, x_ref, wo_ref, n2g_ref, wrh_ref, wrl_ref, br_ref,
                      h1_ref, xg_ref, route_ref, counts_ref, cnt_ref):
    yg = (y_ref[...] * g_ref[...]).astype(BF16)
    h1 = x_ref[...] + _dot(yg, wo_ref[...])
    h1_ref[...] = h1
    _route(h1, n2g_ref[...], wrh_ref[...], wrl_ref[...], br_ref[...], xg_ref, route_ref, counts_ref, cnt_ref,
           pl.program_id(0) == 0)


def _tail_outputs(N, D, tt, tok_map):
    shapes = (jax.ShapeDtypeStruct((N, D), F32), jax.ShapeDtypeStruct((N, D + LANES), F32),
              jax.ShapeDtypeStruct((N, LANES), F32), jax.ShapeDtypeStruct((1, LANES), F32))
    specs = (pl.BlockSpec((tt, D), tok_map), pl.BlockSpec((tt, D + LANES), tok_map),
             pl.BlockSpec((tt, LANES), tok_map), pl.BlockSpec((1, LANES), lambda *_: (0, 0)))
    return shapes, specs


def _rwkv_post(y, g, x, wo, n2g, wr_hi, wr_lo, br, *, tt):
    N, D = x.shape
    tok_map = lambda i: (i, 0)
    tok = pl.BlockSpec((tt, D), tok_map)

    def full(arr):
        nd = arr.ndim
        return pl.BlockSpec(arr.shape, lambda i: (0,) * nd)

    params = (wo, n2g, wr_hi, wr_lo, br)
    out_shape, out_specs = _tail_outputs(N, D, tt, tok_map)
    return pl.pallas_call(
        _rwkv_post_kernel,
        out_shape=out_shape,
        grid=(N // tt,),
        in_specs=[tok, tok, tok] + [full(p) for p in params],
        out_specs=out_specs,
        scratch_shapes=[pltpu.VMEM((8, LANES), F32)],
        compiler_params=pltpu.CompilerParams(
            dimension_semantics=("arbitrary",), vmem_limit_bytes=VMEM_LIMIT),
        name="rwkv_post",
    )(y, g, x, *params)


HALO = 32


def _conformer_kernel(x_ref, n1g_ref, wpw1_ref, bpw1_ref, wdw_ref, bdw_ref, lng_ref, lnb_ref, wpw2_ref, bpw2_ref,
                      n2g_ref, wrh_ref, wrl_ref, br_ref, h1_ref, xg_ref, route_ref, counts_ref, ubuf_ref, cnt_ref):
    i = pl.program_id(1)
    tt, D = x_ref.shape
    x = x_ref[...]
    hn = _rms(x, n1g_ref[...]).astype(BF16)
    u = _dot(hn, wpw1_ref[...]) + bpw1_ref[...]
    u = u[:, :D] * _sigmoid(u[:, D:])

    @pl.when(i == 0)
    def _():
        ubuf_ref[0:HALO, :] = jnp.zeros((HALO, D), F32)
        ubuf_ref[HALO + tt:, :] = jnp.zeros((SUBLANES, D), F32)

    ubuf_ref[HALO:HALO + tt, :] = u
    base = HALO - (CONV_WIDTH - 1)
    acc = jnp.zeros((tt, D), F32) + bdw_ref[...]
    for s in range(SUBLANES):
        part = None
        for q in range((base + CONV_WIDTH - 1) // SUBLANES + 1):
            j = q * SUBLANES + s - base
            if 0 <= j < CONV_WIDTH:
                term = wdw_ref[j:j + 1, :] * ubuf_ref[q * SUBLANES:q * SUBLANES + tt + SUBLANES, :]
                part = term if part is None else part + term
        acc = acc + part[s:s + tt, :]
    ubuf_ref[0:HALO, :] = ubuf_ref[tt:tt + HALO, :]

    m = jnp.mean(acc, axis=-1, keepdims=True)
    c = acc - m
    var = jnp.mean(c * c, axis=-1, keepdims=True)
    c = c * lax.rsqrt(var + CONV_LN_EPS) * lng_ref[...] + lnb_ref[...]
    c = c * _sigmoid(c)
    h1 = x + _dot(c.astype(BF16), wpw2_ref[...]) + bpw2_ref[...]
    h1_ref[...] = h1
    _route(h1, n2g_ref[...], wrh_ref[...], wrl_ref[...], br_ref[...], xg_ref, route_ref, counts_ref, cnt_ref,
           (pl.program_id(0) == 0) & (i == 0))


def _conformer(x, n1g, wpw1, bpw1, wdw, bdw, lng, lnb, wpw2, bpw2, n2g, wr_hi, wr_lo, br, *, tt):
    B, T, D = x.shape
    nt = T // tt
    tok = pl.BlockSpec((None, tt, D), lambda b, i: (b, i, 0))

    def full(arr):
        nd = arr.ndim
        return pl.BlockSpec(arr.shape, lambda b, i: (0,) * nd)

    params = (n1g, wpw1, bpw1, wdw, bdw, lng, lnb, wpw2, bpw2, n2g, wr_hi, wr_lo, br)
    out_shape, out_specs = _tail_outputs(B * T, D, tt, lambda b, i: (b * nt + i, 0))
    return pl.pallas_call(
        _conformer_kernel,
        out_shape=out_shape,
        grid=(B, nt),
        in_specs=[tok] + [full(p) for p in params],
        out_specs=out_specs,
        scratch_shapes=[pltpu.VMEM((HALO + tt + SUBLANES, D), F32), pltpu.VMEM((8, LANES), F32)],
        compiler_params=pltpu.CompilerParams(
            dimension_semantics=("arbitrary", "arbitrary"), vmem_limit_bytes=VMEM_LIMIT),
        name="conformer",
    )(x, *params)


def _dispatch_kernel(pos_ref, pad_ref, xg_ref, xs_ref, zero_ref, sem, zsem):
    tt = xg_ref.shape[0]
    i = pl.program_id(0)
    base = i * tt
    n_seg = 2 * N_GROUPS

    def zero_fill(action):
        for s in range(n_seg):
            n = pl.multiple_of(pad_ref[n_seg + s], SUBLANES)
            start = pl.multiple_of(pad_ref[s], SUBLANES)
            copy = pltpu.make_async_copy(zero_ref.at[pl.ds(0, n)], xs_ref.at[pl.ds(start, n)], zsem)
            pl.when(n > 0)(getattr(copy, action))
        for g in range(N_GROUPS):
            start, n = pad_ref[2 * n_seg + g], pad_ref[2 * n_seg + N_GROUPS + g]
            for u in range(SUBLANES - 1):
                copy = pltpu.make_async_copy(zero_ref.at[pl.ds(0, 1)], xs_ref.at[pl.ds(start + u, 1)], zsem)
                pl.when(u < n)(getattr(copy, action))

    @pl.when(i == 0)
    def _():
        zero_ref[...] = jnp.zeros_like(zero_ref)
        zero_fill("start")

    def issue(jj, carry):
        j0 = pl.multiple_of(jj * SUBLANES, SUBLANES)
        for u in range(SUBLANES):
            src = xg_ref.at[pl.ds(j0, SUBLANES)].at[pl.ds(u, 1)]
            copy = pltpu.make_async_copy(src, xs_ref.at[pl.ds(pos_ref[base + j0 + u], 1)], sem)
            copy.start(priority=u % 2)
        return carry

    lax.fori_loop(0, tt // SUBLANES, issue, 0)
    pltpu.make_async_copy(xg_ref, xs_ref.at[pl.ds(0, tt)], sem).wait()

    @pl.when(i == 0)
    def _():
        zero_fill("wait")


def _dispatch(pos, pad, xg, n_rows, *, tt, tm):
    N, W = xg.shape
    return pl.pallas_call(
        _dispatch_kernel,
        out_shape=jax.ShapeDtypeStruct((n_rows, W), F32),
        grid_spec=pltpu.PrefetchScalarGridSpec(
            num_scalar_prefetch=2,
            grid=(N // tt,),
            in_specs=[pl.BlockSpec((tt, W), lambda i, pos, pad: (i, 0))],
            out_specs=pl.BlockSpec(memory_space=pl.ANY),
            scratch_shapes=[pltpu.VMEM((tm, W), F32), pltpu.SemaphoreType.DMA, pltpu.SemaphoreType.DMA]),
        compiler_params=pltpu.CompilerParams(
            dimension_semantics=("arbitrary",), vmem_limit_bytes=VMEM_LIMIT, disable_bounds_checks=True),
        name="moe_dispatch",
    )(pos, pad, xg)


def _moe_group_kernel(tg_ref, nu_ref, xs_ref, wg_ref, wu_ref, wd_ref, ys_ref, *, eb):
    i = pl.program_id(0)
    e = pl.program_id(1)
    d = xs_ref.shape[1] - LANES

    @pl.when(i < nu_ref[0])
    def _():
        x = xs_ref[:, :d].astype(BF16)
        gates = xs_ref[:, d:]
        lane = lax.broadcasted_iota(jnp.int32, gates.shape, 1)
        first = tg_ref[i] * EXPERTS_PER_GROUP + e * eb
        zs = []
        for q in range(eb):
            ge = jnp.sum(jnp.where(lane == first + q, gates, 0.0), axis=-1, keepdims=True)
            zg = _dot(x, wg_ref[q])
            zu = _dot(x, wu_ref[q])
            zs.append((zg * _sigmoid(zg) * zu * ge).astype(BF16))
        contrib = _dot(jnp.concatenate(zs, axis=-1), wd_ref[...].reshape(-1, d))

        contrib = contrib.reshape(ys_ref.shape)

        @pl.when(e == 0)
        def _():
            ys_ref[...] = contrib

        @pl.when(e != 0)
        def _():
            ys_ref[...] += contrib

    @pl.when((i >= nu_ref[0]) & (e == 0))
    def _():
        ys_ref[...] = jnp.zeros_like(ys_ref)


def _moe_group(tile_group, n_used, xs, wg, wu, wd, *, layer, tm, eb):
    n_rows, W = xs.shape
    _, E, D, F = wg.shape
    steps = EXPERTS_PER_GROUP // eb
    row_map = lambda i, e, tg, nu: (jnp.minimum(i, nu[0] - 1), 0)
    w_map = lambda i, e, tg, nu: (layer, tg[i] * steps + e, 0, 0)
    return pl.pallas_call(
        functools.partial(_moe_group_kernel, eb=eb),
        out_shape=jax.ShapeDtypeStruct((n_rows, D // LANES, LANES), F32),
        grid_spec=pltpu.PrefetchScalarGridSpec(
            num_scalar_prefetch=2,
            grid=(n_rows // tm, steps),
            in_specs=[pl.BlockSpec((tm, W), row_map),
                      pl.BlockSpec((None, eb, D, F), w_map), pl.BlockSpec((None, eb, D, F), w_map),
                      pl.BlockSpec((None, eb, F, D), w_map)],
            out_specs=pl.BlockSpec((tm, D // LANES, LANES), lambda i, e, tg, nu: (i, 0, 0))),
        compiler_params=pltpu.CompilerParams(
            dimension_semantics=("arbitrary", "arbitrary"), vmem_limit_bytes=VMEM_LIMIT),
        name="moe_group",
    )(tile_group, n_used, xs, wg, wu, wd)


def _combine_kernel(pos_ref, h1_ref, fg_ref, ys_ref, out_ref, buf_ref, sem, *, final_norm):
    tt = h1_ref.shape[0]
    i = pl.program_id(0)
    slot = lax.rem(i, 2)

    def gather(tile, s):
        base = tile * tt

        def issue(jj, carry):
            j0 = jj * SUBLANES
            for u in range(SUBLANES):
                dst = buf_ref.at[s].at[pl.ds(j0 + u, 1)]
                copy = pltpu.make_async_copy(ys_ref.at[pl.ds(pos_ref[base + j0 + u], 1)], dst, sem.at[s])
                copy.start(priority=u % 2)
            return carry

        lax.fori_loop(0, tt // SUBLANES, issue, 0)

    @pl.when(i == 0)
    def _():
        gather(0, 0)

    @pl.when(i + 1 < pl.num_programs(0))
    def _():
        gather(i + 1, 1 - slot)

    pltpu.make_async_copy(ys_ref.at[pl.ds(0, tt)], buf_ref.at[slot], sem.at[slot]).wait()
    h = h1_ref[...] + buf_ref[slot].reshape(h1_ref.shape)
    out_ref[...] = _rms(h, fg_ref[...]) if final_norm else h


def _combine(pos, h1, fg, ys, *, tt, final_norm):
    N, D = h1.shape
    tok = pl.BlockSpec((tt, D), lambda i, pos: (i, 0))
    return pl.pallas_call(
        functools.partial(_combine_kernel, final_norm=final_norm),
        out_shape=jax.ShapeDtypeStruct((N, D), F32),
        grid_spec=pltpu.PrefetchScalarGridSpec(
            num_scalar_prefetch=1,
            grid=(N // tt,),
            in_specs=[tok, pl.BlockSpec((1, D), lambda i, pos: (0, 0)), pl.BlockSpec(memory_space=pl.ANY)],
            out_specs=tok,
            scratch_shapes=[pltpu.VMEM((2, tt, D // LANES, LANES), F32), pltpu.SemaphoreType.DMA((2,))]),
        compiler_params=pltpu.CompilerParams(
            dimension_semantics=("arbitrary",), vmem_limit_bytes=VMEM_LIMIT, disable_bounds_checks=True),
        name="moe_combine_final" if final_norm else "moe_combine",
    )(pos, h1, fg, ys)


def _moe(route, counts, xg, h1, wg, wu, wd, fg, *, layer, tm, tt, eb, final_norm):
    N, D = h1.shape
    group = route[:, 0].astype(jnp.int32)
    rank = route[:, 1].astype(jnp.int32)
    cnt = counts[0, :N_GROUPS].astype(jnp.int32)
    padded = (cnt + tm - 1) // tm * tm
    ends = jnp.cumsum(padded)
    pos = rank
    for g in range(N_GROUPS):
        pos = pos + jnp.where(group == g, ends[g] - padded[g], 0)
    n_tiles = N // tm + N_GROUPS
    n_used = ends[-1:] // tm
    tile_group = jnp.sum((jnp.arange(n_tiles)[:, None] * tm >= ends[None, :]).astype(jnp.int32), axis=1)
    tile_group = jnp.minimum(tile_group, tile_group[n_used[0] - 1])
    tail = ends[-1] + jnp.arange(N_GROUPS) * tm
    pad_start, pad_rows = ends - padded + cnt, padded - cnt
    head_rows = jnp.minimum(pad_rows, (-pad_start) % SUBLANES)
    pad = jnp.concatenate([pad_start + head_rows, tail, pad_rows - head_rows, jnp.where(tail < n_tiles * tm, tm, 0),
                           pad_start, head_rows]).astype(jnp.int32)
    xs = _dispatch(pos, pad, xg, n_tiles * tm, tt=tt, tm=tm)
    ys = _moe_group(tile_group, n_used, xs, wg, wu, wd, layer=layer, tm=tm, eb=eb)
    return _combine(pos, h1, fg, ys, tt=_tile(N, 512), final_norm=final_norm)


def _router_weights(w_group, b_group, w_expert, b_expert):
    D = w_group.shape[0]
    w = jnp.zeros((D, LANES), F32).at[:, :N_EXPERTS].set(w_expert).at[:, N_EXPERTS:N_EXPERTS + N_GROUPS].set(w_group)
    b = jnp.zeros((1, LANES), F32).at[0, :N_EXPERTS].set(b_expert).at[0, N_EXPERTS:N_EXPERTS + N_GROUPS].set(b_group)
    hi = w.astype(BF16)
    lo = (w - hi.astype(F32)).astype(BF16)
    return hi, lo, b


def _pad_cols(w, n):
    return jnp.pad(w, ((0, 0), (0, n - w.shape[1])))


def _pad_rows(w, n):
    return jnp.pad(w, ((0, n - w.shape[0]), (0, 0)))


def _tile(n, want):
    t = min(n, want)
    while n % t:
        t //= 2
    return t


def kernel(x, norm1_g, norm2_g, final_g, rwkv_mu, rwkv_w_rkv, rwkv_w0, rwkv_w1, rwkv_w2, rwkv_a0, rwkv_a1, rwkv_a2, rwkv_g1, rwkv_g2, rwkv_k_k, rwkv_k_a, rwkv_r_k, rwkv_ln_g, rwkv_ln_b, rwkv_w_o, conv_w_pw1, conv_b_pw1, conv_w_dw, conv_b_dw, conv_ln_g, conv_ln_b, conv_w_pw2, conv_b_pw2, moe_w_group, moe_b_group, moe_w_expert, moe_b_expert, moe_w_gate, moe_w_up, moe_w_down):
    B, T, D = x.shape
    N = B * T
    depth = norm1_g.shape[0]
    assert depth == 2 and D % QUAD == 0 and T % CHUNK == 0
    row = lambda v: v.reshape(1, -1)
    bf = lambda w: w.astype(BF16)

    r, k, v, wl, a, g = _rwkv_pre(
        x, row(norm1_g[0]), rwkv_mu[0], bf(rwkv_w_rkv[0]), row(rwkv_w0[0]),
        bf(_pad_cols(rwkv_w1[0], LANES)), bf(_pad_rows(rwkv_w2[0], LANES)), row(rwkv_a0[0]),
        bf(_pad_cols(rwkv_a1[0], LANES)), bf(_pad_rows(rwkv_a2[0], LANES)),
        bf(rwkv_g1[0]), bf(rwkv_g2[0]), tt=_tile(T, 512))
    y = _recurrence(r, k, v, wl, a, row(rwkv_k_k[0]), row(rwkv_k_a[0]), rwkv_r_k[0].reshape(1, -1),
                    row(rwkv_ln_g[0]), row(rwkv_ln_b[0]))
    wr_hi, wr_lo, br = _router_weights(moe_w_group[0], moe_b_group[0], moe_w_expert[0], moe_b_expert[0])
    h1, xg, route, counts = _rwkv_post(y.reshape(N, D), g.reshape(N, D), x.reshape(N, D), bf(rwkv_w_o[0]),
                                       row(norm2_g[0]), wr_hi, wr_lo, br, tt=_tile(N, 512))
    moe_tiles = dict(tm=_tile(N, 1024), tt=_tile(N, 1024), eb=EXPERTS_PER_GROUP)
    moe_w = (bf(moe_w_gate), bf(moe_w_up), bf(moe_w_down))
    h = _moe(route, counts, xg, h1, *moe_w, row(final_g),
             layer=0, final_norm=False, **moe_tiles)

    wr_hi, wr_lo, br = _router_weights(moe_w_group[1], moe_b_group[1], moe_w_expert[1], moe_b_expert[1])
    h1, xg, route, counts = _conformer(
        h.reshape(B, T, D), row(norm1_g[1]), bf(conv_w_pw1[0]), row(conv_b_pw1[0]), conv_w_dw[0], row(conv_b_dw[0]),
        row(conv_ln_g[0]), row(conv_ln_b[0]), bf(conv_w_pw2[0]), row(conv_b_pw2[0]),
        row(norm2_g[1]), wr_hi, wr_lo, br, tt=_tile(T, 256))
    out = _moe(route, counts, xg, h1, *moe_w, row(final_g),
               layer=1, final_norm=True, **moe_tiles)
    return out.reshape(B, T, D)
```

```python
import functools
import math

import jax
import jax.numpy as jnp
from jax import lax
from jax.experimental import pallas as pl
from jax.experimental.pallas import tpu as pltpu

F32 = jnp.float32
BF16 = jnp.bfloat16

HEAD = 64
LNX_EPS = HEAD * 1e-5
CONV_WIDTH = 31
CONV_LN_EPS = 1e-5
N_GROUPS = 4
EXPERTS_PER_GROUP = 8
N_EXPERTS = N_GROUPS * EXPERTS_PER_GROUP
RMS_EPS = 1e-6

LANES = 128
SUBLANES = 8
QUAD = 4 * HEAD
CHUNK = 64
VMEM_LIMIT = 56 * 1024 * 1024

NEG_BIG = -1e30


def _dot(a, b):
    return jnp.dot(a, b, preferred_element_type=F32)


def _dot_nt(a, b):
    return lax.dot_general(a, b, (((1,), (1,)), ((), ())), preferred_element_type=F32)


def _dot_tn(a, b):
    return lax.dot_general(a, b, (((0,), (0,)), ((), ())), preferred_element_type=F32)


def _split2(x):
    hi = x.astype(BF16)
    lo = (x - hi.astype(F32)).astype(BF16)
    return hi, lo


def _sigmoid(x):
    return 1.0 / (1.0 + jnp.exp(-x))


def _rms(x, g):
    return x * lax.rsqrt(jnp.mean(x * x, axis=-1, keepdims=True) + RMS_EPS) * g


def _rwkv_pre_kernel(x_ref, n1g_ref, mu_ref, wrkv_ref, w0_ref, w1_ref, w2_ref, a0_ref, a1_ref, a2_ref,
                     g1_ref, g2_ref, r_ref, k_ref, v_ref, wl_ref, a_ref, g_ref, prev_ref):
    i = pl.program_id(1)
    tt = x_ref.shape[0]
    hn = _rms(x_ref[...], n1g_ref[...])

    @pl.when(i == 0)
    def _():
        prev_ref[...] = jnp.zeros_like(prev_ref)

    prev = prev_ref[0:1, :]
    rolled = pltpu.roll(hn, 1, 0)
    row = lax.broadcasted_iota(jnp.int32, hn.shape, 0)
    shifted = jnp.where(row == 0, prev, rolled)
    prev_ref[0:1, :] = hn[tt - 1:tt, :]
    xx = shifted - hn

    def mix(j):
        return (hn + xx * mu_ref[j:j + 1, :]).astype(BF16)

    r_ref[...] = _dot(mix(0), wrkv_ref[0]).astype(BF16)
    k_ref[...] = _dot(mix(2), wrkv_ref[1]).astype(BF16)
    v_ref[...] = _dot(mix(3), wrkv_ref[2]).astype(BF16)
    zw = w0_ref[...] + _dot(jnp.tanh(_dot(mix(1), w1_ref[...])).astype(BF16), w2_ref[...])
    wl_ref[...] = -math.exp(-0.5) * _sigmoid(zw)
    a = _sigmoid(a0_ref[...] + _dot(_dot(mix(4), a1_ref[...]).astype(BF16), a2_ref[...]))
    a_ref[...] = a.astype(BF16)
    g_ref[...] = _dot(_sigmoid(_dot(mix(5), g1_ref[...])).astype(BF16), g2_ref[...]).astype(BF16)


def _rwkv_pre(x, n1g, mu, wrkv, w0, w1, w2, a0, a1, a2, g1, g2, *, tt):
    B, T, D = x.shape
    tok = pl.BlockSpec((None, tt, D), lambda b, i: (b, i, 0))

    def full(arr):
        nd = arr.ndim
        return pl.BlockSpec(arr.shape, lambda b, i: (0,) * nd)

    params = (n1g, mu, wrkv, w0, w1, w2, a0, a1, a2, g1, g2)
    out = lambda dt: jax.ShapeDtypeStruct((B, T, D), dt)
    return pl.pallas_call(
        _rwkv_pre_kernel,
        out_shape=(out(BF16), out(BF16), out(BF16), out(F32), out(BF16), out(BF16)),
        grid=(B, T // tt),
        in_specs=[tok] + [full(p) for p in params],
        out_specs=(tok,) * 6,
        scratch_shapes=[pltpu.VMEM((8, D), F32)],
        compiler_params=pltpu.CompilerParams(
            dimension_semantics=("arbitrary", "arbitrary"), vmem_limit_bytes=VMEM_LIMIT),
        name="rwkv_pre",
    )(x, *params)


def _rec_consts(tb):
    L, Q = CHUNK, QUAD
    ri = lax.broadcasted_iota(jnp.int32, (Q, Q), 0)
    ci = lax.broadcasted_iota(jnp.int32, (Q, Q), 1)
    bd = (jnp.right_shift(ri, 6) == jnp.right_shift(ci, 6)).astype(F32)
    ei = lax.broadcasted_iota(jnp.int32, (L, 4 * L), 0)
    ej = lax.broadcasted_iota(jnp.int32, (L, 4 * L), 1)
    eye_lp = (jnp.bitwise_and(ej, L - 1) == ei).astype(F32)
    mi = lax.broadcasted_iota(jnp.int32, (2 * L, 8 * L), 0)
    mj = lax.broadcasted_iota(jnp.int32, (2 * L, 8 * L), 1)
    t_row = jnp.bitwise_and(mi, L - 1)
    t_col = jnp.bitwise_and(mj, L - 1)
    score_mask = ((t_col < t_row) | ((mi >= L) & (t_col == t_row))).astype(F32)
    bi = lax.broadcasted_iota(jnp.int32, (tb, tb), 0)
    bj = lax.broadcasted_iota(jnp.int32, (tb, tb), 1)
    tri_bd = ((jnp.right_shift(bi, 6) == jnp.right_shift(bj, 6)) & (bj <= bi)).astype(BF16)
    return bd, eye_lp, score_mask, tri_bd


def _recurrence_kernel(r_ref, k_ref, v_ref, wl_ref, a_ref, kk_ref, ka_ref, rk_ref, lg_ref, lb_ref,
                       y_ref, z_ref, rt_s, at_s, kt_s, bt_s, kh_s, bh_s, vb_s, w_s, arb_s, u0_s, y0_s, bon_s,
                       kbt_s, gcol_s):
    assert CHUNK == HEAD and QUAD == 4 * CHUNK
    L, Q = CHUNK, QUAD
    t = pl.program_id(0)
    nc, _, width = r_ref.shape
    tb = nc * L
    nq = width // Q

    def load(ref, cols):
        return jnp.concatenate([ref[b, :, cols].astype(F32) for b in range(nc)], axis=0)
    bd_f, eye_lp, score_mask_f, tri_bd = _rec_consts(tb)
    bd_b = bd_f.astype(BF16)

    def bdsum(x, passes=1):
        if passes == 1:
            return _dot(x.astype(BF16), bd_b)
        hi, lo = _split2(x)
        return _dot(hi, bd_b) + _dot(lo, bd_b)

    def stack(xb):
        return jnp.concatenate([xb] * 4, axis=0) * bd_b

    @pl.when(t == 0)
    def _():
        z_ref[...] = jnp.zeros_like(z_ref)

    for q in range(nq):
        cols = slice(q * Q, (q + 1) * Q)
        r, k, v, wl, a = (load(ref, cols) for ref in (r_ref, k_ref, v_ref, wl_ref, a_ref))
        kk = k * kk_ref[:, cols]
        kk = kk * lax.rsqrt(jnp.maximum(bdsum(kk * kk, passes=2), 1e-24))
        k2 = k * (1.0 + (a - 1.0) * ka_ref[:, cols])
        bb = kk * a
        bon_s[:, cols] = bdsum(r * k2 * rk_ref[:, cols]) * v
        w_hi, w_lo = _split2(wl)
        cw = _dot(tri_bd, w_hi) + _dot(tri_bd, w_lo)
        rt_s[:, cols] = (r * jnp.exp(cw)).astype(BF16)
        at_s[:, cols] = (-kk * jnp.exp(cw - wl)).astype(BF16)
        g_inv = jnp.exp(-cw)
        kt_s[:, cols] = (k2 * g_inv).astype(BF16)
        bt_s[:, cols] = (bb * g_inv).astype(BF16)
        vb_s[:, cols] = v.astype(BF16)
        for c in range(nc):
            rows = slice(c * L, (c + 1) * L)
            tot = cw[(c + 1) * L - 1:(c + 1) * L, :]
            g_rem = jnp.exp(tot - cw[rows, :])
            kh_s[rows, cols] = (k2[rows, :] * g_rem).astype(BF16)
            bh_s[rows, cols] = (bb[rows, :] * g_rem).astype(BF16)
            gcol_s[c * nq + q] = jnp.broadcast_to(jnp.transpose(jnp.exp(tot)), (Q, LANES))

    tiles = [(slice(c * L, (c + 1) * L), slice(q * Q, (q + 1) * Q), c * nq + q)
             for c in range(nc) for q in range(nq)]
    score_mask = score_mask_f > 0.5
    for rows, cols, j in tiles:
        rhs = jnp.concatenate([stack(kt_s[rows, cols]), stack(bt_s[rows, cols])], axis=0)
        lhs = jnp.concatenate([at_s[rows, cols], rt_s[rows, cols]], axis=0)
        sc = jnp.where(score_mask, _dot_nt(lhs, rhs), 0.0)
        kt_s[rows, cols] = sc[:L, :4 * L].astype(BF16)
        bt_s[rows, cols] = sc[L:, :4 * L].astype(BF16)
        arb_s[rows, cols] = sc[L:, 4 * L:].astype(BF16)
        a_ab = sc[:L, 4 * L:]
        w_s[rows, cols] = a_ab.astype(BF16)
        u0_s[rows, cols] = eye_lp + a_ab
        khbh = jnp.concatenate([kh_s[rows, cols], bh_s[rows, cols]], axis=0).astype(F32)
        kbt_s[j] = jnp.transpose(khbh).astype(BF16)
    for rows, cols, j in tiles:
        nb = w_s[rows, cols]
        w_s[rows, cols] = _dot(nb, stack(nb)).astype(BF16)
    for _ in range(int(math.log2(L)) - 2):
        for rows, cols, j in tiles:
            mb = w_s[rows, cols]
            tinv = u0_s[rows, cols]
            res = _dot(jnp.concatenate([mb, tinv.astype(BF16)], axis=0), stack(mb))
            w_s[rows, cols] = res[:L].astype(BF16)
            u0_s[rows, cols] = tinv + res[L:]
    for rows, cols, j in tiles:
        lhs = jnp.concatenate([kt_s[rows, cols], bt_s[rows, cols]], axis=0)
        pu_y0 = _dot(lhs, stack(vb_s[rows, cols]))
        kt_s[rows, cols] = pu_y0[:L].astype(BF16)
        y0_s[rows, cols] = pu_y0[L:]
    for rows, cols, j in tiles:
        tinv = u0_s[rows, cols]
        tinv = tinv + _dot(tinv.astype(BF16), stack(w_s[rows, cols]))
        wu = _dot(tinv.astype(BF16),
                  jnp.concatenate([stack(at_s[rows, cols]), stack(kt_s[rows, cols])], axis=1))
        w_s[rows, cols] = wu[:, :Q].astype(BF16)
        u0_s[rows, cols] = wu[:, Q:]

    for rows, cols, j in tiles:
        z = z_ref[j]
        wr = _dot(jnp.concatenate([w_s[rows, cols], rt_s[rows, cols]], axis=0), z.astype(BF16))
        usb = (wr[:L] + u0_s[rows, cols]).astype(BF16)
        y0_s[rows, cols] = y0_s[rows, cols] + wr[L:] + _dot(arb_s[rows, cols], stack(usb))
        upd = _dot(kbt_s[j], jnp.concatenate([vb_s[rows, cols], usb], axis=0))
        g = gcol_s[j]
        z_ref[j] = z * jnp.concatenate([g, g], axis=1) + upd * bd_f

    for q in range(nq):
        cols = slice(q * Q, (q + 1) * Q)
        y = y0_s[:, cols]
        yc = y - bdsum(y) * (1.0 / HEAD)
        var = bdsum(yc * yc) * (1.0 / HEAD)
        out = bon_s[:, cols] + yc * lax.rsqrt(var + LNX_EPS) * lg_ref[:, cols] + lb_ref[:, cols]
        for b in range(nc):
            y_ref[b, :, cols] = out[b * L:(b + 1) * L, :].astype(y_ref.dtype)


def _recurrence(r, k, v, wl, a, k_k, k_a, r_k, ln_g, ln_b):
    B, T, D = r.shape
    tok = pl.BlockSpec((B, CHUNK, D), lambda t: (0, t, 0))
    par = pl.BlockSpec((1, D), lambda t: (0, 0))
    n_tiles = B * (D // QUAD)
    blk_bf = pltpu.VMEM((B * CHUNK, D), BF16)
    blk_f32 = pltpu.VMEM((B * CHUNK, D), F32)
    return pl.pallas_call(
        _recurrence_kernel,
        out_shape=jax.ShapeDtypeStruct((B, T, D), BF16),
        grid=(T // CHUNK,),
        in_specs=[tok] * 5 + [par] * 5,
        out_specs=tok,
        scratch_shapes=[pltpu.VMEM((n_tiles, QUAD, QUAD), F32)] + [blk_bf] * 9 + [blk_f32] * 3
        + [pltpu.VMEM((n_tiles, QUAD, 2 * CHUNK), BF16), pltpu.VMEM((n_tiles, QUAD, LANES), F32)],
        compiler_params=pltpu.CompilerParams(
            dimension_semantics=("arbitrary",), vmem_limit_bytes=VMEM_LIMIT),
        name="rwkv_recurrence",
    )(r, k, v, wl, a, k_k, k_a, r_k, ln_g, ln_b)


def _route(h1, n2g, wr_hi, wr_lo, br, xg_ref, route_ref, counts_ref, cnt_ref, is_first):
    d = h1.shape[1]
    hn = _rms(h1, n2g)
    x_hi, x_lo = _split2(hn)
    logits = _dot(x_hi, wr_hi) + _dot(x_lo, wr_hi) + _dot(x_hi, wr_lo) + br
    lane = lax.broadcasted_iota(jnp.int32, logits.shape, 1)

    def first_argmax(vals):
        mx = jnp.max(vals, axis=-1, keepdims=True)
        idx = jnp.min(jnp.where(vals == mx, lane, 4 * LANES), axis=-1, keepdims=True)
        return mx, idx

    lg = jnp.where((lane >= N_EXPERTS) & (lane < N_EXPERTS + N_GROUPS), logits, NEG_BIG)
    g_max, g_lane = first_argmax(lg)
    p_top = 1.0 / jnp.sum(jnp.exp(lg - g_max), axis=-1, keepdims=True)
    g_idx = g_lane - N_EXPERTS
    in_group = (lane >= g_idx * EXPERTS_PER_GROUP) & (lane < (g_idx + 1) * EXPERTS_PER_GROUP)
    le = jnp.where(in_group, logits, NEG_BIG)
    m1, i1 = first_argmax(le)
    m2, i2 = first_argmax(jnp.where(lane == i1, NEG_BIG, le))
    e2 = jnp.exp(m2 - m1)
    w1 = p_top / (1.0 + e2)
    w2 = p_top * e2 / (1.0 + e2)
    gates = jnp.where(lane == i1, w1, jnp.where(lane == i2, w2, 0.0))
    xg_ref[:, :d] = hn
    xg_ref[:, d:] = gates

    @pl.when(is_first)
    def _():
        cnt_ref[...] = jnp.zeros_like(cnt_ref)

    rows = h1.shape[0]
    onehot = lane == g_idx
    ri = lax.broadcasted_iota(jnp.int32, (rows, rows), 0)
    ci = lax.broadcasted_iota(jnp.int32, (rows, rows), 1)
    before = _dot((ci < ri).astype(F32).astype(BF16), onehot.astype(F32).astype(BF16)) + cnt_ref[0:1, :]
    rank = jnp.sum(jnp.where(onehot, before, 0.0), axis=-1, keepdims=True)
    route_ref[...] = jnp.where(lane == 0, g_idx.astype(F32), jnp.where(lane == 1, rank, 0.0))
    total = cnt_ref[0:1, :] + jnp.sum(onehot.astype(F32), axis=0, keepdims=True)
    cnt_ref[0:1, :] = total
    counts_ref[...] = total


def _rwkv_post_kernel(y_ref, g_ref, x_ref, wo_ref, n2g_ref, wrh_ref, wrl_ref, br_ref,
                      h1_ref, xg_ref, route_ref, counts_ref, cnt_ref):
    yg = (y_ref[...] * g_ref[...]).astype(BF16)
    h1 = x_ref[...] + _dot(yg, wo_ref[...])
    h1_ref[...] = h1
    _route(h1, n2g_ref[...], wrh_ref[...], wrl_ref[...], br_ref[...], xg_ref, route_ref, counts_ref, cnt_ref,
           pl.program_id(0) == 0)


def _tail_outputs(N, D, tt, tok_map):
    shapes = (jax.ShapeDtypeStruct((N, D), F32), jax.ShapeDtypeStruct((N, D + LANES), F32),
              jax.ShapeDtypeStruct((N, LANES), F32), jax.ShapeDtypeStruct((1, LANES), F32))
    specs = (pl.BlockSpec((tt, D), tok_map), pl.BlockSpec((tt, D + LANES), tok_map),
             pl.BlockSpec((tt, LANES), tok_map), pl.BlockSpec((1, LANES), lambda *_: (0, 0)))
    return shapes, specs


def _rwkv_post(y, g, x, wo, n2g, wr_hi, wr_lo, br, *, tt):
    N, D = x.shape
    tok_map = lambda i: (i, 0)
    tok = pl.BlockSpec((tt, D), tok_map)

    def full(arr):
        nd = arr.ndim
        return pl.BlockSpec(arr.shape, lambda i: (0,) * nd)

    params = (wo, n2g, wr_hi, wr_lo, br)
    out_shape, out_specs = _tail_outputs(N, D, tt, tok_map)
    return pl.pallas_call(
        _rwkv_post_kernel,
        out_shape=out_shape,
        grid=(N // tt,),
        in_specs=[tok, tok, tok] + [full(p) for p in params],
        out_specs=out_specs,
        scratch_shapes=[pltpu.VMEM((8, LANES), F32)],
        compiler_params=pltpu.CompilerParams(
            dimension_semantics=("arbitrary",), vmem_limit_bytes=VMEM_LIMIT),
        name="rwkv_post",
    )(y, g, x, *params)


HALO = 32


def _conformer_kernel(x_ref, n1g_ref, wpw1_ref, bpw1_ref, wdw_ref, bdw_ref, lng_ref, lnb_ref, wpw2_ref, bpw2_ref,
                      n2g_ref, wrh_ref, wrl_ref, br_ref, h1_ref, xg_ref, route_ref, counts_ref, ubuf_ref, cnt_ref):
    i = pl.program_id(1)
    tt, D = x_ref.shape
    x = x_ref[...]
    hn = _rms(x, n1g_ref[...]).astype(BF16)
    u = _dot(hn, wpw1_ref[...]) + bpw1_ref[...]
    u = u[:, :D] * _sigmoid(u[:, D:])

    @pl.when(i == 0)
    def _():
        ubuf_ref[0:HALO, :] = jnp.zeros((HALO, D), F32)
        ubuf_ref[HALO + tt:, :] = jnp.zeros((SUBLANES, D), F32)

    ubuf_ref[HALO:HALO + tt, :] = u
    base = HALO - (CONV_WIDTH - 1)
    acc = jnp.zeros((tt, D), F32) + bdw_ref[...]
    for s in range(SUBLANES):
        part = None
        for q in range((base + CONV_WIDTH - 1) // SUBLANES + 1):
            j = q * SUBLANES + s - base
            if 0 <= j < CONV_WIDTH:
                term = wdw_ref[j:j + 1, :] * ubuf_ref[q * SUBLANES:q * SUBLANES + tt + SUBLANES, :]
                part = term if part is None else part + term
        acc = acc + part[s:s + tt, :]
    ubuf_ref[0:HALO, :] = ubuf_ref[tt:tt + HALO, :]

    m = jnp.mean(acc, axis=-1, keepdims=True)
    c = acc - m
    var = jnp.mean(c * c, axis=-1, keepdims=True)
    c = c * lax.rsqrt(var + CONV_LN_EPS) * lng_ref[...] + lnb_ref[...]
    c = c * _sigmoid(c)
    h1 = x + _dot(c.astype(BF16), wpw2_ref[...]) + bpw2_ref[...]
    h1_ref[...] = h1
    _route(h1, n2g_ref[...], wrh_ref[...], wrl_ref[...], br_ref[...], xg_ref, route_ref, counts_ref, cnt_ref,
           (pl.program_id(0) == 0) & (i == 0))


def _conformer(x, n1g, wpw1, bpw1, wdw, bdw, lng, lnb, wpw2, bpw2, n2g, wr_hi, wr_lo, br, *, tt):
    B, T, D = x.shape
    nt = T // tt
    tok = pl.BlockSpec((None, tt, D), lambda b, i: (b, i, 0))

    def full(arr):
        nd = arr.ndim
        return pl.BlockSpec(arr.shape, lambda b, i: (0,) * nd)

    params = (n1g, wpw1, bpw1, wdw, bdw, lng, lnb, wpw2, bpw2, n2g, wr_hi, wr_lo, br)
    out_shape, out_specs = _tail_outputs(B * T, D, tt, lambda b, i: (b * nt + i, 0))
    return pl.pallas_call(
        _conformer_kernel,
        out_shape=out_shape,
        grid=(B, nt),
        in_specs=[tok] + [full(p) for p in params],
        out_specs=out_specs,
        scratch_shapes=[pltpu.VMEM((HALO + tt + SUBLANES, D), F32), pltpu.VMEM((8, LANES), F32)],
        compiler_params=pltpu.CompilerParams(
            dimension_semantics=("arbitrary", "arbitrary"), vmem_limit_bytes=VMEM_LIMIT),
        name="conformer",
    )(x, *params)


def _dispatch_kernel(pos_ref, pad_ref, xg_ref, xs_ref, zero_ref, sem, zsem):
    tt = xg_ref.shape[0]
    i = pl.program_id(0)
    base = i * tt
    n_seg = 2 * N_GROUPS

    def zero_fill(action):
        for s in range(n_seg):
            n = pl.multiple_of(pad_ref[n_seg + s], SUBLANES)
            start = pl.multiple_of(pad_ref[s], SUBLANES)
            copy = pltpu.make_async_copy(zero_ref.at[pl.ds(0, n)], xs_ref.at[pl.ds(start, n)], zsem)
            pl.when(n > 0)(getattr(copy, action))
        for g in range(N_GROUPS):
            start, n = pad_ref[2 * n_seg + g], pad_ref[2 * n_seg + N_GROUPS + g]
            for u in range(SUBLANES - 1):
                copy = pltpu.make_async_copy(zero_ref.at[pl.ds(0, 1)], xs_ref.at[pl.ds(start + u, 1)], zsem)
                pl.when(u < n)(getattr(copy, action))

    @pl.when(i == 0)
    def _():
        zero_ref[...] = jnp.zeros_like(zero_ref)
        zero_fill("start")

    def issue(jj, carry):
        j0 = pl.multiple_of(jj * SUBLANES, SUBLANES)
        for u in range(SUBLANES):
            src = xg_ref.at[pl.ds(j0, SUBLANES)].at[pl.ds(u, 1)]
            copy = pltpu.make_async_copy(src, xs_ref.at[pl.ds(pos_ref[base + j0 + u], 1)], sem)
            copy.start(priority=u % 2)
        return carry

    lax.fori_loop(0, tt // SUBLANES, issue, 0)
    pltpu.make_async_copy(xg_ref, xs_ref.at[pl.ds(0, tt)], sem).wait()

    @pl.when(i == 0)
    def _():
        zero_fill("wait")


def _dispatch(pos, pad, xg, n_rows, *, tt, tm):
    N, W = xg.shape
    return pl.pallas_call(
        _dispatch_kernel,
        out_shape=jax.ShapeDtypeStruct((n_rows, W), F32),
        grid_spec=pltpu.PrefetchScalarGridSpec(
            num_scalar_prefetch=2,
            grid=(N // tt,),
            in_specs=[pl.BlockSpec((tt, W), lambda i, pos, pad: (i, 0))],
            out_specs=pl.BlockSpec(memory_space=pl.ANY),
            scratch_shapes=[pltpu.VMEM((tm, W), F32), pltpu.SemaphoreType.DMA, pltpu.SemaphoreType.DMA]),
        compiler_params=pltpu.CompilerParams(
            dimension_semantics=("arbitrary",), vmem_limit_bytes=VMEM_LIMIT, disable_bounds_checks=True),
        name="moe_dispatch",
    )(pos, pad, xg)


def _moe_group_kernel(tg_ref, nu_ref, xs_ref, wg_ref, wu_ref, wd_ref, ys_ref, *, eb):
    i = pl.program_id(0)
    e = pl.program_id(1)
    d = xs_ref.shape[1] - LANES

    @pl.when(i < nu_ref[0])
    def _():
        x = xs_ref[:, :d].astype(BF16)
        gates = xs_ref[:, d:]
        lane = lax.broadcasted_iota(jnp.int32, gates.shape, 1)
        first = tg_ref[i] * EXPERTS_PER_GROUP + e * eb
        zs = []
        for q in range(eb):
            ge = jnp.sum(jnp.where(lane == first + q, gates, 0.0), axis=-1, keepdims=True)
            zg = _dot(x, wg_ref[q])
            zu = _dot(x, wu_ref[q])
            zs.append((zg * _sigmoid(zg) * zu * ge).astype(BF16))
        contrib = _dot(jnp.concatenate(zs, axis=-1), wd_ref[...].reshape(-1, d))

        contrib = contrib.reshape(ys_ref.shape)

        @pl.when(e == 0)
        def _():
            ys_ref[...] = contrib

        @pl.when(e != 0)
        def _():
            ys_ref[...] += contrib

    @pl.when((i >= nu_ref[0]) & (e == 0))
    def _():
        ys_ref[...] = jnp.zeros_like(ys_ref)


def _moe_group(tile_group, n_used, xs, wg, wu, wd, *, layer, tm, eb):
    n_rows, W = xs.shape
    _, E, D, F = wg.shape
    steps = EXPERTS_PER_GROUP // eb
    row_map = lambda i, e, tg, nu: (jnp.minimum(i, nu[0] - 1), 0)
    w_map = lambda i, e, tg, nu: (layer, tg[i] * steps + e, 0, 0)
    return pl.pallas_call(
        functools.partial(_moe_group_kernel, eb=eb),
        out_shape=jax.ShapeDtypeStruct((n_rows, D // LANES, LANES), F32),
        grid_spec=pltpu.PrefetchScalarGridSpec(
            num_scalar_prefetch=2,
            grid=(n_rows // tm, steps),
            in_specs=[pl.BlockSpec((tm, W), row_map),
                      pl.BlockSpec((None, eb, D, F), w_map), pl.BlockSpec((None, eb, D, F), w_map),
                      pl.BlockSpec((None, eb, F, D), w_map)],
            out_specs=pl.BlockSpec((tm, D // LANES, LANES), lambda i, e, tg, nu: (i, 0, 0))),
        compiler_params=pltpu.CompilerParams(
            dimension_semantics=("arbitrary", "arbitrary"), vmem_limit_bytes=VMEM_LIMIT),
        name="moe_group",
    )(tile_group, n_used, xs, wg, wu, wd)


def _combine_kernel(pos_ref, h1_ref, fg_ref, ys_ref, out_ref, buf_ref, sem, *, final_norm):
    tt = h1_ref.shape[0]
    i = pl.program_id(0)
    slot = lax.rem(i, 2)

    def gather(tile, s):
        base = tile * tt

        def issue(jj, carry):
            j0 = jj * SUBLANES
            for u in range(SUBLANES):
                dst = buf_ref.at[s].at[pl.ds(j0 + u, 1)]
                copy = pltpu.make_async_copy(ys_ref.at[pl.ds(pos_ref[base + j0 + u], 1)], dst, sem.at[s])
                copy.start(priority=u % 2)
            return carry

        lax.fori_loop(0, tt // SUBLANES, issue, 0)

    @pl.when(i == 0)
    def _():
        gather(0, 0)

    @pl.when(i + 1 < pl.num_programs(0))
    def _():
        gather(i + 1, 1 - slot)

    pltpu.make_async_copy(ys_ref.at[pl.ds(0, tt)], buf_ref.at[slot], sem.at[slot]).wait()
    h = h1_ref[...] + buf_ref[slot].reshape(h1_ref.shape)
    out_ref[...] = _rms(h, fg_ref[...]) if final_norm else h


def _combine(pos, h1, fg, ys, *, tt, final_norm):
    N, D = h1.shape
    tok = pl.BlockSpec((tt, D), lambda i, pos: (i, 0))
    return pl.pallas_call(
        functools.partial(_combine_kernel, final_norm=final_norm),
        out_shape=jax.ShapeDtypeStruct((N, D), F32),
        grid_spec=pltpu.PrefetchScalarGridSpec(
            num_scalar_prefetch=1,
            grid=(N // tt,),
            in_specs=[tok, pl.BlockSpec((1, D), lambda i, pos: (0, 0)), pl.BlockSpec(memory_space=pl.ANY)],
            out_specs=tok,
            scratch_shapes=[pltpu.VMEM((2, tt, D // LANES, LANES), F32), pltpu.SemaphoreType.DMA((2,))]),
        compiler_params=pltpu.CompilerParams(
            dimension_semantics=("arbitrary",), vmem_limit_bytes=VMEM_LIMIT, disable_bounds_checks=True),
        name="moe_combine_final" if final_norm else "moe_combine",
    )(pos, h1, fg, ys)


def _moe(route, counts, xg, h1, wg, wu, wd, fg, *, layer, tm, tt, eb, final_norm):
    N, D = h1.shape
    group = route[:, 0].astype(jnp.int32)
    rank = route[:, 1].astype(jnp.int32)
    cnt = counts[0, :N_GROUPS].astype(jnp.int32)
    padded = (cnt + tm - 1) // tm * tm
    ends = jnp.cumsum(padded)
    pos = rank
    for g in range(N_GROUPS):
        pos = pos + jnp.where(group == g, ends[g] - padded[g], 0)
    n_tiles = N // tm + N_GROUPS
    n_used = ends[-1:] // tm
    tile_group = jnp.sum((jnp.arange(n_tiles)[:, None] * tm >= ends[None, :]).astype(jnp.int32), axis=1)
    tile_group = jnp.minimum(tile_group, tile_group[n_used[0] - 1])
    tail = ends[-1] + jnp.arange(N_GROUPS) * tm
    pad_start, pad_rows = ends - padded + cnt, padded - cnt
    head_rows = jnp.minimum(pad_rows, (-pad_start) % SUBLANES)
    pad = jnp.concatenate([pad_start + head_rows, tail, pad_rows - head_rows, jnp.where(tail < n_tiles * tm, tm, 0),
                           pad_start, head_rows]).astype(jnp.int32)
    xs = _dispatch(pos, pad, xg, n_tiles * tm, tt=tt, tm=tm)
    ys = _moe_group(tile_group, n_used, xs, wg, wu, wd, layer=layer, tm=tm, eb=eb)
    return _combine(pos, h1, fg, ys, tt=_tile(N, 512), final_norm=final_norm)


def _router_weights(w_group, b_group, w_expert, b_expert):
    D = w_group.shape[0]
    w = jnp.zeros((D, LANES), F32).at[:, :N_EXPERTS].set(w_expert).at[:, N_EXPERTS:N_EXPERTS + N_GROUPS].set(w_group)
    b = jnp.zeros((1, LANES), F32).at[0, :N_EXPERTS].set(b_expert).at[0, N_EXPERTS:N_EXPERTS + N_GROUPS].set(b_group)
    hi = w.astype(BF16)
    lo = (w - hi.astype(F32)).astype(BF16)
    return hi, lo, b


def _pad_cols(w, n):
    return jnp.pad(w, ((0, 0), (0, n - w.shape[1])))


def _pad_rows(w, n):
    return jnp.pad(w, ((0, n - w.shape[0]), (0, 0)))


def _tile(n, want):
    t = min(n, want)
    while n % t:
        t //= 2
    return t


def kernel(x, norm1_g, norm2_g, final_g, rwkv_mu, rwkv_w_rkv, rwkv_w0, rwkv_w1, rwkv_w2, rwkv_a0, rwkv_a1, rwkv_a2, rwkv_g1, rwkv_g2, rwkv_k_k, rwkv_k_a, rwkv_r_k, rwkv_ln_g, rwkv_ln_b, rwkv_w_o, conv_w_pw1, conv_b_pw1, conv_w_dw, conv_b_dw, conv_ln_g, conv_ln_b, conv_w_pw2, conv_b_pw2, moe_w_group, moe_b_group, moe_w_expert, moe_b_expert, moe_w_gate, moe_w_up, moe_w_down):
    B, T, D = x.shape
    N = B * T
    depth = norm1_g.shape[0]
    assert depth == 2 and D % QUAD == 0 and T % CHUNK == 0
    row = lambda v: v.reshape(1, -1)
    bf = lambda w: w.astype(BF16)

    r, k, v, wl, a, g = _rwkv_pre(
        x, row(norm1_g[0]), rwkv_mu[0], bf(rwkv_w_rkv[0]), row(rwkv_w0[0]),
        bf(_pad_cols(rwkv_w1[0], LANES)), bf(_pad_rows(rwkv_w2[0], LANES)), row(rwkv_a0[0]),
        bf(_pad_cols(rwkv_a1[0], LANES)), bf(_pad_rows(rwkv_a2[0], LANES)),
        bf(rwkv_g1[0]), bf(rwkv_g2[0]), tt=_tile(T, 512))
    y = _recurrence(r, k, v, wl, a, row(rwkv_k_k[0]), row(rwkv_k_a[0]), rwkv_r_k[0].reshape(1, -1),
                    row(rwkv_ln_g[0]), row(rwkv_ln_b[0]))
    wr_hi, wr_lo, br = _router_weights(moe_w_group[0], moe_b_group[0], moe_w_expert[0], moe_b_expert[0])
    h1, xg, route, counts = _rwkv_post(y.reshape(N, D), g.reshape(N, D), x.reshape(N, D), bf(rwkv_w_o[0]),
                                       row(norm2_g[0]), wr_hi, wr_lo, br, tt=_tile(N, 512))
    moe_tiles = dict(tm=_tile(N, 1024), tt=_tile(N, 1024), eb=EXPERTS_PER_GROUP)
    moe_w = (bf(moe_w_gate), bf(moe_w_up), bf(moe_w_down))
    h = _moe(route, counts, xg, h1, *moe_w, row(final_g),
             layer=0, final_norm=False, **moe_tiles)

    wr_hi, wr_lo, br = _router_weights(moe_w_group[1], moe_b_group[1], moe_w_expert[1], moe_b_expert[1])
    h1, xg, route, counts = _conformer(
        h.reshape(B, T, D), row(norm1_g[1]), bf(conv_w_pw1[0]), row(conv_b_pw1[0]), conv_w_dw[0], row(conv_b_dw[0]),
        row(conv_ln_g[0]), row(conv_ln_b[0]), bf(conv_w_pw2[0]), row(conv_b_pw2[0]),
        row(norm2_g[1]), wr_hi, wr_lo, br, tt=_tile(T, 512))
    out = _moe(route, counts, xg, h1, *moe_w, row(final_g),
               layer=1, final_norm=True, **moe_tiles)
    return out.reshape(B, T, D)
```

```python
import functools
import math

import jax
import jax.numpy as jnp
from jax import lax
from jax.experimental import pallas as pl
from jax.experimental.pallas import tpu as pltpu

F32 = jnp.float32
BF16 = jnp.bfloat16

HEAD = 64
LNX_EPS = HEAD * 1e-5
CONV_WIDTH = 31
CONV_LN_EPS = 1e-5
N_GROUPS = 4
EXPERTS_PER_GROUP = 8
N_EXPERTS = N_GROUPS * EXPERTS_PER_GROUP
RMS_EPS = 1e-6

LANES = 128
SUBLANES = 8
QUAD = 4 * HEAD
CHUNK = 64
VMEM_LIMIT = 56 * 1024 * 1024

NEG_BIG = -1e30


def _dot(a, b):
    return jnp.dot(a, b, preferred_element_type=F32)


def _dot_nt(a, b):
    return lax.dot_general(a, b, (((1,), (1,)), ((), ())), preferred_element_type=F32)


def _dot_tn(a, b):
    return lax.dot_general(a, b, (((0,), (0,)), ((), ())), preferred_element_type=F32)


def _split2(x):
    hi = x.astype(BF16)
    lo = (x - hi.astype(F32)).astype(BF16)
    return hi, lo


def _sigmoid(x):
    return 1.0 / (1.0 + jnp.exp(-x))


def _rms(x, g):
    return x * lax.rsqrt(jnp.mean(x * x, axis=-1, keepdims=True) + RMS_EPS) * g


def _rwkv_pre_kernel(x_ref, n1g_ref, mu_ref, wrkv_ref, w0_ref, w1_ref, w2_ref, a0_ref, a1_ref, a2_ref,
                     g1_ref, g2_ref, r_ref, k_ref, v_ref, wl_ref, a_ref, g_ref, prev_ref):
    i = pl.program_id(1)
    tt = x_ref.shape[0]
    hn = _rms(x_ref[...], n1g_ref[...])

    @pl.when(i == 0)
    def _():
        prev_ref[...] = jnp.zeros_like(prev_ref)

    prev = prev_ref[0:1, :]
    rolled = pltpu.roll(hn, 1, 0)
    row = lax.broadcasted_iota(jnp.int32, hn.shape, 0)
    shifted = jnp.where(row == 0, prev, rolled)
    prev_ref[0:1, :] = hn[tt - 1:tt, :]
    xx = shifted - hn

    def mix(j):
        return (hn + xx * mu_ref[j:j + 1, :]).astype(BF16)

    r_ref[...] = _dot(mix(0), wrkv_ref[0]).astype(BF16)
    k_ref[...] = _dot(mix(2), wrkv_ref[1]).astype(BF16)
    v_ref[...] = _dot(mix(3), wrkv_ref[2]).astype(BF16)
    zw = w0_ref[...] + _dot(jnp.tanh(_dot(mix(1), w1_ref[...])).astype(BF16), w2_ref[...])
    wl_ref[...] = -math.exp(-0.5) * _sigmoid(zw)
    a = _sigmoid(a0_ref[...] + _dot(_dot(mix(4), a1_ref[...]).astype(BF16), a2_ref[...]))
    a_ref[...] = a.astype(BF16)
    g_ref[...] = _dot(_sigmoid(_dot(mix(5), g1_ref[...])).astype(BF16), g2_ref[...]).astype(BF16)


def _rwkv_pre(x, n1g, mu, wrkv, w0, w1, w2, a0, a1, a2, g1, g2, *, tt):
    B, T, D = x.shape
    tok = pl.BlockSpec((None, tt, D), lambda b, i: (b, i, 0))

    def full(arr):
        nd = arr.ndim
        return pl.BlockSpec(arr.shape, lambda b, i: (0,) * nd)

    params = (n1g, mu, wrkv, w0, w1, w2, a0, a1, a2, g1, g2)
    out = lambda dt: jax.ShapeDtypeStruct((B, T, D), dt)
    return pl.pallas_call(
        _rwkv_pre_kernel,
        out_shape=(out(BF16), out(BF16), out(BF16), out(F32), out(BF16), out(BF16)),
        grid=(B, T // tt),
        in_specs=[tok] + [full(p) for p in params],
        out_specs=(tok,) * 6,
        scratch_shapes=[pltpu.VMEM((8, D), F32)],
        compiler_params=pltpu.CompilerParams(
            dimension_semantics=("arbitrary", "arbitrary"), vmem_limit_bytes=VMEM_LIMIT),
        name="rwkv_pre",
    )(x, *params)


def _rec_consts(tb):
    L, Q = CHUNK, QUAD
    ri = lax.broadcasted_iota(jnp.int32, (Q, Q), 0)
    ci = lax.broadcasted_iota(jnp.int32, (Q, Q), 1)
    bd = (jnp.right_shift(ri, 6) == jnp.right_shift(ci, 6)).astype(F32)
    ei = lax.broadcasted_iota(jnp.int32, (L, 4 * L), 0)
    ej = lax.broadcasted_iota(jnp.int32, (L, 4 * L), 1)
    eye_lp = (jnp.bitwise_and(ej, L - 1) == ei).astype(F32)
    mi = lax.broadcasted_iota(jnp.int32, (2 * L, 8 * L), 0)
    mj = lax.broadcasted_iota(jnp.int32, (2 * L, 8 * L), 1)
    t_row = jnp.bitwise_and(mi, L - 1)
    t_col = jnp.bitwise_and(mj, L - 1)
    score_mask = ((t_col < t_row) | ((mi >= L) & (t_col == t_row))).astype(F32)
    bi = lax.broadcasted_iota(jnp.int32, (tb, tb), 0)
    bj = lax.broadcasted_iota(jnp.int32, (tb, tb), 1)
    tri_bd = ((jnp.right_shift(bi, 6) == jnp.right_shift(bj, 6)) & (bj <= bi)).astype(BF16)
    return bd, eye_lp, score_mask, tri_bd


def _recurrence_kernel(r_ref, k_ref, v_ref, wl_ref, a_ref, kk_ref, ka_ref, rk_ref, lg_ref, lb_ref,
                       y_ref, z_ref, rt_s, at_s, kt_s, bt_s, kh_s, bh_s, vb_s, w_s, arb_s, u0_s, y0_s, bon_s,
                       kbt_s, gcol_s):
    assert CHUNK == HEAD and QUAD == 4 * CHUNK
    L, Q = CHUNK, QUAD
    t = pl.program_id(0)
    nc, _, width = r_ref.shape
    tb = nc * L
    nq = width // Q

    def load(ref, cols):
        return jnp.concatenate([ref[b, :, cols].astype(F32) for b in range(nc)], axis=0)
    bd_f, eye_lp, score_mask_f, tri_bd = _rec_consts(tb)
    bd_b = bd_f.astype(BF16)

    def bdsum(x, passes=1):
        if passes == 1:
            return _dot(x.astype(BF16), bd_b)
        hi, lo = _split2(x)
        return _dot(hi, bd_b) + _dot(lo, bd_b)

    def stack(xb):
        return jnp.concatenate([xb] * 4, axis=0) * bd_b

    @pl.when(t == 0)
    def _():
        z_ref[...] = jnp.zeros_like(z_ref)

    for q in range(nq):
        cols = slice(q * Q, (q + 1) * Q)
        r, k, v, wl, a = (load(ref, cols) for ref in (r_ref, k_ref, v_ref, wl_ref, a_ref))
        kk = k * kk_ref[:, cols]
        kk = kk * lax.rsqrt(jnp.maximum(bdsum(kk * kk, passes=2), 1e-24))
        k2 = k * (1.0 + (a - 1.0) * ka_ref[:, cols])
        bb = kk * a
        bon_s[:, cols] = bdsum(r * k2 * rk_ref[:, cols]) * v
        w_hi, w_lo = _split2(wl)
        cw = _dot(tri_bd, w_hi) + _dot(tri_bd, w_lo)
        rt_s[:, cols] = (r * jnp.exp(cw)).astype(BF16)
        at_s[:, cols] = (-kk * jnp.exp(cw - wl)).astype(BF16)
        g_inv = jnp.exp(-cw)
        kt_s[:, cols] = (k2 * g_inv).astype(BF16)
        bt_s[:, cols] = (bb * g_inv).astype(BF16)
        vb_s[:, cols] = v.astype(BF16)
        for c in range(nc):
            rows = slice(c * L, (c + 1) * L)
            tot = cw[(c + 1) * L - 1:(c + 1) * L, :]
            g_rem = jnp.exp(tot - cw[rows, :])
            kh_s[rows, cols] = (k2[rows, :] * g_rem).astype(BF16)
            bh_s[rows, cols] = (bb[rows, :] * g_rem).astype(BF16)
            gcol_s[c * nq + q] = jnp.broadcast_to(jnp.transpose(jnp.exp(tot)), (Q, LANES))

    tiles = [(slice(c * L, (c + 1) * L), slice(q * Q, (q + 1) * Q), c * nq + q)
             for c in range(nc) for q in range(nq)]
    score_mask = score_mask_f > 0.5
    for rows, cols, j in tiles:
        rhs = jnp.concatenate([stack(kt_s[rows, cols]), stack(bt_s[rows, cols])], axis=0)
        lhs = jnp.concatenate([at_s[rows, cols], rt_s[rows, cols]], axis=0)
        sc = jnp.where(score_mask, _dot_nt(lhs, rhs), 0.0)
        kt_s[rows, cols] = sc[:L, :4 * L].astype(BF16)
        bt_s[rows, cols] = sc[L:, :4 * L].astype(BF16)
        arb_s[rows, cols] = sc[L:, 4 * L:].astype(BF16)
        a_ab = sc[:L, 4 * L:]
        w_s[rows, cols] = a_ab.astype(BF16)
        u0_s[rows, cols] = eye_lp + a_ab
        khbh = jnp.concatenate([kh_s[rows, cols], bh_s[rows, cols]], axis=0).astype(F32)
        kbt_s[j] = jnp.transpose(khbh).astype(BF16)
    for rows, cols, j in tiles:
        nb = w_s[rows, cols]
        w_s[rows, cols] = _dot(nb, stack(nb)).astype(BF16)
    for _ in range(int(math.log2(L)) - 2):
        for rows, cols, j in tiles:
            mb = w_s[rows, cols]
            tinv = u0_s[rows, cols]
            res = _dot(jnp.concatenate([mb, tinv.astype(BF16)], axis=0), stack(mb))
            w_s[rows, cols] = res[:L].astype(BF16)
            u0_s[rows, cols] = tinv + res[L:]
    for rows, cols, j in tiles:
        lhs = jnp.concatenate([kt_s[rows, cols], bt_s[rows, cols]], axis=0)
        pu_y0 = _dot(lhs, stack(vb_s[rows, cols]))
        kt_s[rows, cols] = pu_y0[:L].astype(BF16)
        y0_s[rows, cols] = pu_y0[L:]
    for rows, cols, j in tiles:
        tinv = u0_s[rows, cols]
        tinv = tinv + _dot(tinv.astype(BF16), stack(w_s[rows, cols]))
        wu = _dot(tinv.astype(BF16),
                  jnp.concatenate([stack(at_s[rows, cols]), stack(kt_s[rows, cols])], axis=1))
        w_s[rows, cols] = wu[:, :Q].astype(BF16)
        u0_s[rows, cols] = wu[:, Q:]

    for rows, cols, j in tiles:
        z = z_ref[j]
        wr = _dot(jnp.concatenate([w_s[rows, cols], rt_s[rows, cols]], axis=0), z.astype(BF16))
        usb = (wr[:L] + u0_s[rows, cols]).astype(BF16)
        y0_s[rows, cols] = y0_s[rows, cols] + wr[L:] + _dot(arb_s[rows, cols], stack(usb))
        upd = _dot(kbt_s[j], jnp.concatenate([vb_s[rows, cols], usb], axis=0))
        g = gcol_s[j]
        z_ref[j] = z * jnp.concatenate([g, g], axis=1) + upd * bd_f

    for q in range(nq):
        cols = slice(q * Q, (q + 1) * Q)
        y = y0_s[:, cols]
        yc = y - bdsum(y) * (1.0 / HEAD)
        var = bdsum(yc * yc) * (1.0 / HEAD)
        out = bon_s[:, cols] + yc * lax.rsqrt(var + LNX_EPS) * lg_ref[:, cols] + lb_ref[:, cols]
        for b in range(nc):
            y_ref[b, :, cols] = out[b * L:(b + 1) * L, :].astype(y_ref.dtype)


def _recurrence(r, k, v, wl, a, k_k, k_a, r_k, ln_g, ln_b):
    B, T, D = r.shape
    tok = pl.BlockSpec((B, CHUNK, D), lambda t: (0, t, 0))
    par = pl.BlockSpec((1, D), lambda t: (0, 0))
    n_tiles = B * (D // QUAD)
    blk_bf = pltpu.VMEM((B * CHUNK, D), BF16)
    blk_f32 = pltpu.VMEM((B * CHUNK, D), F32)
    return pl.pallas_call(
        _recurrence_kernel,
        out_shape=jax.ShapeDtypeStruct((B, T, D), BF16),
        grid=(T // CHUNK,),
        in_specs=[tok] * 5 + [par] * 5,
        out_specs=tok,
        scratch_shapes=[pltpu.VMEM((n_tiles, QUAD, QUAD), F32)] + [blk_bf] * 9 + [blk_f32] * 3
        + [pltpu.VMEM((n_tiles, QUAD, 2 * CHUNK), BF16), pltpu.VMEM((n_tiles, QUAD, LANES), F32)],
        compiler_params=pltpu.CompilerParams(
            dimension_semantics=("arbitrary",), vmem_limit_bytes=VMEM_LIMIT),
        name="rwkv_recurrence",
    )(r, k, v, wl, a, k_k, k_a, r_k, ln_g, ln_b)


def _route(h1, n2g, wr_hi, wr_lo, br, xg_ref, route_ref, counts_ref, cnt_ref, is_first):
    d = h1.shape[1]
    hn = _rms(h1, n2g)
    x_hi, x_lo = _split2(hn)
    logits = _dot(x_hi, wr_hi) + _dot(x_lo, wr_hi) + _dot(x_hi, wr_lo) + br
    lane = lax.broadcasted_iota(jnp.int32, logits.shape, 1)

    def first_argmax(vals):
        mx = jnp.max(vals, axis=-1, keepdims=True)
        idx = jnp.min(jnp.where(vals == mx, lane, 4 * LANES), axis=-1, keepdims=True)
        return mx, idx

    lg = jnp.where((lane >= N_EXPERTS) & (lane < N_EXPERTS + N_GROUPS), logits, NEG_BIG)
    g_max, g_lane = first_argmax(lg)
    p_top = 1.0 / jnp.sum(jnp.exp(lg - g_max), axis=-1, keepdims=True)
    g_idx = g_lane - N_EXPERTS
    in_group = (lane >= g_idx * EXPERTS_PER_GROUP) & (lane < (g_idx + 1) * EXPERTS_PER_GROUP)
    le = jnp.where(in_group, logits, NEG_BIG)
    m1, i1 = first_argmax(le)
    m2, i2 = first_argmax(jnp.where(lane == i1, NEG_BIG, le))
    e2 = jnp.exp(m2 - m1)
    w1 = p_top / (1.0 + e2)
    w2 = p_top * e2 / (1.0 + e2)
    gates = jnp.where(lane == i1, w1, jnp.where(lane == i2, w2, 0.0))
    xg_ref[:, :d] = hn
    xg_ref[:, d:] = gates

    @pl.when(is_first)
    def _():
        cnt_ref[...] = jnp.zeros_like(cnt_ref)

    rows = h1.shape[0]
    onehot = lane == g_idx
    ri = lax.broadcasted_iota(jnp.int32, (rows, rows), 0)
    ci = lax.broadcasted_iota(jnp.int32, (rows, rows), 1)
    before = _dot((ci < ri).astype(F32).astype(BF16), onehot.astype(F32).astype(BF16)) + cnt_ref[0:1, :]
    rank = jnp.sum(jnp.where(onehot, before, 0.0), axis=-1, keepdims=True)
    route_ref[...] = jnp.where(lane == 0, g_idx.astype(F32), jnp.where(lane == 1, rank, 0.0))
    total = cnt_ref[0:1, :] + jnp.sum(onehot.astype(F32), axis=0, keepdims=True)
    cnt_ref[0:1, :] = total
    counts_ref[...] = total


def _rwkv_post_kernel(y_ref, g_ref, x_ref, wo_ref, n2g_ref, wrh_ref, wrl_ref, br_ref,
                      h1_ref, xg_ref, route_ref, counts_ref, cnt_ref):
    yg = (y_ref[...] * g_ref[...]).astype(BF16)
    h1 = x_ref[...] + _dot(yg, wo_ref[...])
    h1_ref[...] = h1
    _route(h1, n2g_ref[...], wrh_ref[...], wrl_ref[...], br_ref[...], xg_ref, route_ref, counts_ref, cnt_ref,
           pl.program_id(0) == 0)


def _tail_outputs(N, D, tt, tok_map):
    shapes = (jax.ShapeDtypeStruct((N, D), F32), jax.ShapeDtypeStruct((N, D + LANES), F32),
              jax.ShapeDtypeStruct((N, LANES), F32), jax.ShapeDtypeStruct((1, LANES), F32))
    specs = (pl.BlockSpec((tt, D), tok_map), pl.BlockSpec((tt, D + LANES), tok_map),
             pl.BlockSpec((tt, LANES), tok_map), pl.BlockSpec((1, LANES), lambda *_: (0, 0)))
    return shapes, specs


def _rwkv_post(y, g, x, wo, n2g, wr_hi, wr_lo, br, *, tt):
    N, D = x.shape
    tok_map = lambda i: (i, 0)
    tok = pl.BlockSpec((tt, D), tok_map)

    def full(arr):
        nd = arr.ndim
        return pl.BlockSpec(arr.shape, lambda i: (0,) * nd)

    params = (wo, n2g, wr_hi, wr_lo, br)
    out_shape, out_specs = _tail_outputs(N, D, tt, tok_map)
    return pl.pallas_call(
        _rwkv_post_kernel,
        out_shape=out_shape,
        grid=(N // tt,),
        in_specs=[tok, tok, tok] + [full(p) for p in params],
        out_specs=out_specs,
        scratch_shapes=[pltpu.VMEM((8, LANES), F32)],
        compiler_params=pltpu.CompilerParams(
            dimension_semantics=("arbitrary",), vmem_limit_bytes=VMEM_LIMIT),
        name="rwkv_post",
    )(y, g, x, *params)


HALO = 32


def _conformer_kernel(x_ref, n1g_ref, wpw1_ref, bpw1_ref, wdw_ref, bdw_ref, lng_ref, lnb_ref, wpw2_ref, bpw2_ref,
                      n2g_ref, wrh_ref, wrl_ref, br_ref, h1_ref, xg_ref, route_ref, counts_ref, ubuf_ref, cnt_ref):
    i = pl.program_id(1)
    tt, D = x_ref.shape
    x = x_ref[...]
    hn = _rms(x, n1g_ref[...]).astype(BF16)
    u = _dot(hn, wpw1_ref[...]) + bpw1_ref[...]
    u = u[:, :D] * _sigmoid(u[:, D:])

    @pl.when(i == 0)
    def _():
        ubuf_ref[0:HALO, :] = jnp.zeros((HALO, D), F32)
        ubuf_ref[HALO + tt:, :] = jnp.zeros((SUBLANES, D), F32)

    ubuf_ref[HALO:HALO + tt, :] = u
    base = HALO - (CONV_WIDTH - 1)
    acc = jnp.zeros((tt, D), F32) + bdw_ref[...]
    for s in range(SUBLANES):
        part = None
        for q in range((base + CONV_WIDTH - 1) // SUBLANES + 1):
            j = q * SUBLANES + s - base
            if 0 <= j < CONV_WIDTH:
                term = wdw_ref[j:j + 1, :] * ubuf_ref[q * SUBLANES:q * SUBLANES + tt + SUBLANES, :]
                part = term if part is None else part + term
        acc = acc + part[s:s + tt, :]
    ubuf_ref[0:HALO, :] = ubuf_ref[tt:tt + HALO, :]

    m = jnp.mean(acc, axis=-1, keepdims=True)
    c = acc - m
    var = jnp.mean(c * c, axis=-1, keepdims=True)
    c = c * lax.rsqrt(var + CONV_LN_EPS) * lng_ref[...] + lnb_ref[...]
    c = c * _sigmoid(c)
    h1 = x + _dot(c.astype(BF16), wpw2_ref[...]) + bpw2_ref[...]
    h1_ref[...] = h1
    _route(h1, n2g_ref[...], wrh_ref[...], wrl_ref[...], br_ref[...], xg_ref, route_ref, counts_ref, cnt_ref,
           (pl.program_id(0) == 0) & (i == 0))


def _conformer(x, n1g, wpw1, bpw1, wdw, bdw, lng, lnb, wpw2, bpw2, n2g, wr_hi, wr_lo, br, *, tt):
    B, T, D = x.shape
    nt = T // tt
    tok = pl.BlockSpec((None, tt, D), lambda b, i: (b, i, 0))

    def full(arr):
        nd = arr.ndim
        return pl.BlockSpec(arr.shape, lambda b, i: (0,) * nd)

    params = (n1g, wpw1, bpw1, wdw, bdw, lng, lnb, wpw2, bpw2, n2g, wr_hi, wr_lo, br)
    out_shape, out_specs = _tail_outputs(B * T, D, tt, lambda b, i: (b * nt + i, 0))
    return pl.pallas_call(
        _conformer_kernel,
        out_shape=out_shape,
        grid=(B, nt),
        in_specs=[tok] + [full(p) for p in params],
        out_specs=out_specs,
        scratch_shapes=[pltpu.VMEM((HALO + tt + SUBLANES, D), F32), pltpu.VMEM((8, LANES), F32)],
        compiler_params=pltpu.CompilerParams(
            dimension_semantics=("arbitrary", "arbitrary"), vmem_limit_bytes=VMEM_LIMIT),
        name="conformer",
    )(x, *params)


def _dispatch_kernel(pos_ref, pad_ref, xg_ref, xs_ref, zero_ref, sem, zsem):
    tt = xg_ref.shape[0]
    i = pl.program_id(0)
    base = i * tt
    n_seg = 2 * N_GROUPS

    def zero_fill(action):
        for s in range(n_seg):
            n = pl.multiple_of(pad_ref[n_seg + s], SUBLANES)
            start = pl.multiple_of(pad_ref[s], SUBLANES)
            copy = pltpu.make_async_copy(zero_ref.at[pl.ds(0, n)], xs_ref.at[pl.ds(start, n)], zsem)
            pl.when(n > 0)(getattr(copy, action))
        for g in range(N_GROUPS):
            start, n = pad_ref[2 * n_seg + g], pad_ref[2 * n_seg + N_GROUPS + g]
            for u in range(SUBLANES - 1):
                copy = pltpu.make_async_copy(zero_ref.at[pl.ds(0, 1)], xs_ref.at[pl.ds(start + u, 1)], zsem)
                pl.when(u < n)(getattr(copy, action))

    @pl.when(i == 0)
    def _():
        zero_ref[...] = jnp.zeros_like(zero_ref)
        zero_fill("start")

    def issue(jj, carry):
        j0 = pl.multiple_of(jj * SUBLANES, SUBLANES)
        for u in range(SUBLANES):
            src = xg_ref.at[pl.ds(j0, SUBLANES)].at[pl.ds(u, 1)]
            copy = pltpu.make_async_copy(src, xs_ref.at[pl.ds(pos_ref[base + j0 + u], 1)], sem)
            copy.start(priority=u % 2)
        return carry

    lax.fori_loop(0, tt // SUBLANES, issue, 0)
    pltpu.make_async_copy(xg_ref, xs_ref.at[pl.ds(0, tt)], sem).wait()

    @pl.when(i == 0)
    def _():
        zero_fill("wait")


def _dispatch(pos, pad, xg, n_rows, *, tt, tm):
    N, W = xg.shape
    return pl.pallas_call(
        _dispatch_kernel,
        out_shape=jax.ShapeDtypeStruct((n_rows, W), F32),
        grid_spec=pltpu.PrefetchScalarGridSpec(
            num_scalar_prefetch=2,
            grid=(N // tt,),
            in_specs=[pl.BlockSpec((tt, W), lambda i, pos, pad: (i, 0))],
            out_specs=pl.BlockSpec(memory_space=pl.ANY),
            scratch_shapes=[pltpu.VMEM((tm, W), F32), pltpu.SemaphoreType.DMA, pltpu.SemaphoreType.DMA]),
        compiler_params=pltpu.CompilerParams(
            dimension_semantics=("arbitrary",), vmem_limit_bytes=VMEM_LIMIT, disable_bounds_checks=True),
        name="moe_dispatch",
    )(pos, pad, xg)


def _moe_group_kernel(tg_ref, nu_ref, xs_ref, wg_ref, wu_ref, wd_ref, ys_ref, *, eb):
    i = pl.program_id(0)
    e = pl.program_id(1)
    d = xs_ref.shape[1] - LANES

    @pl.when(i < nu_ref[0])
    def _():
        x = xs_ref[:, :d].astype(BF16)
        gates = xs_ref[:, d:]
        lane = lax.broadcasted_iota(jnp.int32, gates.shape, 1)
        first = tg_ref[i] * EXPERTS_PER_GROUP + e * eb
        zs = []
        for q in range(eb):
            ge = jnp.sum(jnp.where(lane == first + q, gates, 0.0), axis=-1, keepdims=True)
            zg = _dot(x, wg_ref[q])
            zu = _dot(x, wu_ref[q])
            zs.append((zg * _sigmoid(zg) * zu * ge).astype(BF16))
        contrib = _dot(jnp.concatenate(zs, axis=-1), wd_ref[...].reshape(-1, d))

        contrib = contrib.reshape(ys_ref.shape)

        @pl.when(e == 0)
        def _():
            ys_ref[...] = contrib

        @pl.when(e != 0)
        def _():
            ys_ref[...] += contrib

    @pl.when((i >= nu_ref[0]) & (e == 0))
    def _():
        ys_ref[...] = jnp.zeros_like(ys_ref)


def _moe_group(tile_group, n_used, xs, wg, wu, wd, *, layer, tm, eb):
    n_rows, W = xs.shape
    _, E, D, F = wg.shape
    steps = EXPERTS_PER_GROUP // eb
    row_map = lambda i, e, tg, nu: (jnp.minimum(i, nu[0] - 1), 0)
    w_map = lambda i, e, tg, nu: (layer, tg[i] * steps + e, 0, 0)
    return pl.pallas_call(
        functools.partial(_moe_group_kernel, eb=eb),
        out_shape=jax.ShapeDtypeStruct((n_rows, D // LANES, LANES), F32),
        grid_spec=pltpu.PrefetchScalarGridSpec(
            num_scalar_prefetch=2,
            grid=(n_rows // tm, steps),
            in_specs=[pl.BlockSpec((tm, W), row_map),
                      pl.BlockSpec((None, eb, D, F), w_map), pl.BlockSpec((None, eb, D, F), w_map),
                      pl.BlockSpec((None, eb, F, D), w_map)],
            out_specs=pl.BlockSpec((tm, D // LANES, LANES), lambda i, e, tg, nu: (i, 0, 0))),
        compiler_params=pltpu.CompilerParams(
            dimension_semantics=("arbitrary", "arbitrary"), vmem_limit_bytes=VMEM_LIMIT,
            allow_input_fusion=[False, False, False, True, True, True]),
        name="moe_group",
    )(tile_group, n_used, xs, wg, wu, wd)


def _combine_kernel(pos_ref, h1_ref, fg_ref, ys_ref, out_ref, buf_ref, sem, *, final_norm):
    tt = h1_ref.shape[0]
    i = pl.program_id(0)
    slot = lax.rem(i, 2)

    def gather(tile, s):
        base = tile * tt

        def issue(jj, carry):
            j0 = jj * SUBLANES
            for u in range(SUBLANES):
                dst = buf_ref.at[s].at[pl.ds(j0 + u, 1)]
                copy = pltpu.make_async_copy(ys_ref.at[pl.ds(pos_ref[base + j0 + u], 1)], dst, sem.at[s])
                copy.start(priority=u % 2)
            return carry

        lax.fori_loop(0, tt // SUBLANES, issue, 0)

    @pl.when(i == 0)
    def _():
        gather(0, 0)

    @pl.when(i + 1 < pl.num_programs(0))
    def _():
        gather(i + 1, 1 - slot)

    pltpu.make_async_copy(ys_ref.at[pl.ds(0, tt)], buf_ref.at[slot], sem.at[slot]).wait()
    h = h1_ref[...] + buf_ref[slot].reshape(h1_ref.shape)
    out_ref[...] = _rms(h, fg_ref[...]) if final_norm else h


def _combine(pos, h1, fg, ys, *, tt, final_norm):
    N, D = h1.shape
    tok = pl.BlockSpec((tt, D), lambda i, pos: (i, 0))
    return pl.pallas_call(
        functools.partial(_combine_kernel, final_norm=final_norm),
        out_shape=jax.ShapeDtypeStruct((N, D), F32),
        grid_spec=pltpu.PrefetchScalarGridSpec(
            num_scalar_prefetch=1,
            grid=(N // tt,),
            in_specs=[tok, pl.BlockSpec((1, D), lambda i, pos: (0, 0)), pl.BlockSpec(memory_space=pl.ANY)],
            out_specs=tok,
            scratch_shapes=[pltpu.VMEM((2, tt, D // LANES, LANES), F32), pltpu.SemaphoreType.DMA((2,))]),
        compiler_params=pltpu.CompilerParams(
            dimension_semantics=("arbitrary",), vmem_limit_bytes=VMEM_LIMIT, disable_bounds_checks=True),
        name="moe_combine_final" if final_norm else "moe_combine",
    )(pos, h1, fg, ys)


def _moe(route, counts, xg, h1, wg, wu, wd, fg, *, layer, tm, tt, eb, final_norm):
    N, D = h1.shape
    group = route[:, 0].astype(jnp.int32)
    rank = route[:, 1].astype(jnp.int32)
    cnt = counts[0, :N_GROUPS].astype(jnp.int32)
    padded = (cnt + tm - 1) // tm * tm
    ends = jnp.cumsum(padded)
    pos = rank
    for g in range(N_GROUPS):
        pos = pos + jnp.where(group == g, ends[g] - padded[g], 0)
    n_tiles = N // tm + N_GROUPS
    n_used = ends[-1:] // tm
    tile_group = jnp.sum((jnp.arange(n_tiles)[:, None] * tm >= ends[None, :]).astype(jnp.int32), axis=1)
    tile_group = jnp.minimum(tile_group, tile_group[n_used[0] - 1])
    tail = ends[-1] + jnp.arange(N_GROUPS) * tm
    pad_start, pad_rows = ends - padded + cnt, padded - cnt
    head_rows = jnp.minimum(pad_rows, (-pad_start) % SUBLANES)
    pad = jnp.concatenate([pad_start + head_rows, tail, pad_rows - head_rows, jnp.where(tail < n_tiles * tm, tm, 0),
                           pad_start, head_rows]).astype(jnp.int32)
    xs = _dispatch(pos, pad, xg, n_tiles * tm, tt=tt, tm=tm)
    ys = _moe_group(tile_group, n_used, xs, wg, wu, wd, layer=layer, tm=tm, eb=eb)
    return _combine(pos, h1, fg, ys, tt=_tile(N, 512), final_norm=final_norm)


def _router_weights(w_group, b_group, w_expert, b_expert):
    D = w_group.shape[0]
    w = jnp.zeros((D, LANES), F32).at[:, :N_EXPERTS].set(w_expert).at[:, N_EXPERTS:N_EXPERTS + N_GROUPS].set(w_group)
    b = jnp.zeros((1, LANES), F32).at[0, :N_EXPERTS].set(b_expert).at[0, N_EXPERTS:N_EXPERTS + N_GROUPS].set(b_group)
    hi = w.astype(BF16)
    lo = (w - hi.astype(F32)).astype(BF16)
    return hi, lo, b


def _pad_cols(w, n):
    return jnp.pad(w, ((0, 0), (0, n - w.shape[1])))


def _pad_rows(w, n):
    return jnp.pad(w, ((0, n - w.shape[0]), (0, 0)))


def _tile(n, want):
    t = min(n, want)
    while n % t:
        t //= 2
    return t


def kernel(x, norm1_g, norm2_g, final_g, rwkv_mu, rwkv_w_rkv, rwkv_w0, rwkv_w1, rwkv_w2, rwkv_a0, rwkv_a1, rwkv_a2, rwkv_g1, rwkv_g2, rwkv_k_k, rwkv_k_a, rwkv_r_k, rwkv_ln_g, rwkv_ln_b, rwkv_w_o, conv_w_pw1, conv_b_pw1, conv_w_dw, conv_b_dw, conv_ln_g, conv_ln_b, conv_w_pw2, conv_b_pw2, moe_w_group, moe_b_group, moe_w_expert, moe_b_expert, moe_w_gate, moe_w_up, moe_w_down):
    B, T, D = x.shape
    N = B * T
    depth = norm1_g.shape[0]
    assert depth == 2 and D % QUAD == 0 and T % CHUNK == 0
    row = lambda v: v.reshape(1, -1)
    bf = lambda w: w.astype(BF16)

    r, k, v, wl, a, g = _rwkv_pre(
        x, row(norm1_g[0]), rwkv_mu[0], bf(rwkv_w_rkv[0]), row(rwkv_w0[0]),
        bf(_pad_cols(rwkv_w1[0], LANES)), bf(_pad_rows(rwkv_w2[0], LANES)), row(rwkv_a0[0]),
        bf(_pad_cols(rwkv_a1[0], LANES)), bf(_pad_rows(rwkv_a2[0], LANES)),
        bf(rwkv_g1[0]), bf(rwkv_g2[0]), tt=_tile(T, 512))
    y = _recurrence(r, k, v, wl, a, row(rwkv_k_k[0]), row(rwkv_k_a[0]), rwkv_r_k[0].reshape(1, -1),
                    row(rwkv_ln_g[0]), row(rwkv_ln_b[0]))
    wr_hi, wr_lo, br = _router_weights(moe_w_group[0], moe_b_group[0], moe_w_expert[0], moe_b_expert[0])
    h1, xg, route, counts = _rwkv_post(y.reshape(N, D), g.reshape(N, D), x.reshape(N, D), bf(rwkv_w_o[0]),
                                       row(norm2_g[0]), wr_hi, wr_lo, br, tt=_tile(N, 512))
    moe_tiles = dict(tm=_tile(N, 1024), tt=_tile(N, 1024), eb=EXPERTS_PER_GROUP)
    moe_w = (bf(moe_w_gate), bf(moe_w_up), bf(moe_w_down))
    h = _moe(route, counts, xg, h1, *moe_w, row(final_g),
             layer=0, final_norm=False, **moe_tiles)

    wr_hi, wr_lo, br = _router_weights(moe_w_group[1], moe_b_group[1], moe_w_expert[1], moe_b_expert[1])
    h1, xg, route, counts = _conformer(
        h.reshape(B, T, D), row(norm1_g[1]), bf(conv_w_pw1[0]), row(conv_b_pw1[0]), conv_w_dw[0], row(conv_b_dw[0]),
        row(conv_ln_g[0]), row(conv_ln_b[0]), bf(conv_w_pw2[0]), row(conv_b_pw2[0]),
        row(norm2_g[1]), wr_hi, wr_lo, br, tt=_tile(T, 512))
    out = _moe(route, counts, xg, h1, *moe_w, row(final_g),
               layer=1, final_norm=True, **moe_tiles)
    return out.reshape(B, T, D)
```
